```python
import jax, jax.numpy as jnp
from jax import lax
import numpy as np

D_MODEL = 2048
BATCH = 8
SEQ = 2048
DEPTH = 2
DEC_BATCH = 128
DEC_SEQ = 1
PAST_LEN = 16384
PAGE_SIZE = 128

M_HEADS = 4
M_DQK = 128
M_DV = 256
M_WIDTH = M_HEADS * M_DV
M_CHUNK = 64
A_HEADS = 8
A_NOPE = 128
A_ROPE = 64
A_VD = 128
A_WIDTH = A_HEADS * A_VD
Q_RANK = 512
KV_RANK = 256
ROPE_THETA = 10000.0
MLA_SCALE = (A_NOPE + A_ROPE) ** -0.5
Q_BLOCK = 128
MIX_WIDTH = M_WIDTH + A_WIDTH
IN_SPLITS = (M_HEADS * M_DQK, M_HEADS * M_DQK, M_WIDTH, M_WIDTH, M_HEADS, M_HEADS, Q_RANK, KV_RANK, A_ROPE)
N_IN = 2 * M_HEADS * M_DQK + 2 * M_WIDTH + 2 * M_HEADS + Q_RANK + KV_RANK + A_ROPE
D_FF = 5632
N_EXPERTS = 8
TOP_K = 2
D_FF_EXPERT = 5632
EXPERT_BLOCK = 128
N_DENSE = (DEPTH + 1) // 2
N_MOE = DEPTH // 2
PLE_DIM = 256
EPS = 1e-6

kernel_name = 'hymba_mlstm_mla_moe_decode_step'


def rmsnorm(x, g):
    xf = x.astype(jnp.float32)
    y = xf * lax.rsqrt(jnp.mean(xf * xf, axis=-1, keepdims=True) + EPS)
    return (y * g.astype(jnp.float32)).astype(x.dtype)


def rope(x, pos):
    half = A_ROPE // 2
    inv = ROPE_THETA ** (-jnp.arange(half, dtype=jnp.float32) / half)
    ang = pos.astype(jnp.float32)[:, None] * inv[None, :]
    ang = ang.reshape((ang.shape[0],) + (1,) * (x.ndim - 3) + (half,))
    cos, sin = jnp.cos(ang), jnp.sin(ang)
    xf = x.astype(jnp.float32)
    x1, x2 = xf[..., :half], xf[..., half:]
    return jnp.concatenate([x1 * cos - x2 * sin, x2 * cos + x1 * sin], axis=-1).astype(x.dtype)


def split_in(z):
    idx = np.cumsum(IN_SPLITS)[:-1].tolist()
    return jnp.split(z, idx, axis=-1)


def mlstm_chunkwise(q, k, v, i_pre, f_pre, C0, n0, m0):
    f32 = jnp.float32
    b, s = q.shape[:2]
    L = M_CHUNK if s % M_CHUNK == 0 else s
    nc = s // L

    def chunks(a):
        return jnp.swapaxes(a.astype(f32).reshape((b, nc, L) + a.shape[2:]), 0, 1)

    xs = (chunks(q.astype(f32) * (M_DQK ** -0.5)), chunks(k), chunks(v), chunks(i_pre),
          chunks(jax.nn.log_sigmoid(f_pre.astype(f32))))
    causal = jnp.tril(jnp.ones((L, L), dtype=bool))

    def step(carry, inp):
        C, n, m = carry
        qc, kc, vc, ic, lfc = inp
        bh = jnp.swapaxes(jnp.cumsum(lfc, axis=1), 1, 2)
        ih = jnp.swapaxes(ic, 1, 2)
        dmat = jnp.where(causal, bh[..., :, None] - bh[..., None, :] + ih[..., None, :], -jnp.inf)
        inter = bh + m[..., None]
        m_t = jnp.maximum(inter, jnp.max(dmat, axis=-1))
        w_inter = jnp.exp(inter - m_t)
        s_qk = jnp.einsum('bthk,bshk->bhts', qc, kc) * jnp.exp(dmat - m_t[..., None])
        num = (jnp.einsum('bhts,bshv->bthv', s_qk, vc)
               + jnp.einsum('bthk,bhkv->bthv', qc, C) * jnp.swapaxes(w_inter, 1, 2)[..., None])
        den = jnp.sum(s_qk, axis=-1) + w_inter * jnp.einsum('bthk,bhk->bht', qc, n)
        den = jnp.maximum(jnp.abs(den), jnp.exp(-m_t))
        h = num / jnp.swapaxes(den, 1, 2)[..., None]
        m_last = m_t[..., -1]
        g_inter = jnp.exp(bh[..., -1] + m - m_last)
        g_intra = jnp.exp(bh[..., -1:] - bh + ih - m_last[..., None])
        kw = kc * jnp.swapaxes(g_intra, 1, 2)[..., None]
        C_new = g_inter[..., None, None] * C + jnp.einsum('bshk,bshv->bhkv', kw, vc)
        n_new = g_inter[..., None] * n + jnp.sum(kw, axis=1)
        return (C_new, n_new, m_last), h

    (C1, n1, m1), hs = lax.scan(step, (C0.astype(f32), n0.astype(f32), m0.astype(f32)), xs)
    h = jnp.swapaxes(hs, 0, 1).reshape(b, s, M_HEADS, M_DV)
    return h, C1, n1, m1


def mla_attend(q_nope, q_pe, k_nope, k_pe, c_kv, w_uv_h, valid):
    f32 = jnp.float32
    s = (jnp.einsum('bqhn,bthn->bhqt', q_nope.astype(f32), k_nope.astype(f32))
         + jnp.einsum('bqhr,btr->bhqt', q_pe.astype(f32), k_pe.astype(f32))) * MLA_SCALE
    p = jax.nn.softmax(jnp.where(valid, s, -jnp.inf), axis=-1)
    o_lat = jnp.einsum('bhqt,btc->bqhc', p, c_kv.astype(f32))
    return jnp.einsum('bqhc,chv->bqhv', o_lat, w_uv_h.astype(f32))


def swiglu(x, wg, wu, wd):
    return (jax.nn.silu(x @ wg) * (x @ wu)) @ wd


def moe_swiglu(x, w_router, wg, wu, wd):
    f32 = jnp.float32
    shp = x.shape
    xt = x.reshape(-1, shp[-1])
    T = xt.shape[0]
    logits = xt.astype(f32) @ w_router.astype(f32)
    top_v, top_i = lax.top_k(logits, TOP_K)
    gate = jax.nn.softmax(top_v, axis=-1)
    flat_e = top_i.reshape(-1)
    flat_g = gate.reshape(-1)
    flat_t = jnp.arange(T * TOP_K) // TOP_K
    order = jnp.argsort(flat_e)
    se, st, sg = flat_e[order], flat_t[order], flat_g[order]
    counts = jnp.bincount(flat_e, length=N_EXPERTS)
    padded = (counts + EXPERT_BLOCK - 1) // EXPERT_BLOCK * EXPERT_BLOCK
    pend = jnp.cumsum(padded)
    pstart = pend - padded
    sstart = jnp.cumsum(counts) - counts
    dest = pstart[se] + jnp.arange(T * TOP_K) - sstart[se]
    n_blocks = -(-(T * TOP_K) // EXPERT_BLOCK) + N_EXPERTS
    P = n_blocks * EXPERT_BLOCK
    buf_t = jnp.full((P,), T, dtype=jnp.int32).at[dest].set(st.astype(jnp.int32))
    buf_g = jnp.zeros((P,), f32).at[dest].set(sg)
    blk_e = jnp.minimum(jnp.searchsorted(pend, jnp.arange(n_blocks) * EXPERT_BLOCK, side='right'), N_EXPERTS - 1)
    xp = jnp.concatenate([xt, jnp.zeros((1, xt.shape[1]), xt.dtype)], axis=0)
    xb = xp[buf_t].reshape(n_blocks, EXPERT_BLOCK, xt.shape[1])

    def expert_block(args):
        xblk, e = args
        return swiglu(xblk, wg[e], wu[e], wd[e])

    yb = lax.map(expert_block, (xb, blk_e)).reshape(P, xt.shape[1])
    y = jnp.zeros((T + 1, xt.shape[1]), f32).at[buf_t].add(yb.astype(f32) * buf_g[:, None])[:T]
    return y.astype(x.dtype).reshape(shp)


def setup_inputs(seed: int = 0) -> dict:
    key = jax.random.key(seed)
    ks = iter(jax.random.split(key, 64))
    f32 = jnp.float32

    def nrm(shape, scale=1.0):
        return jax.random.normal(next(ks), shape, f32) * scale

    def gain(shape):
        return 1.0 + nrm(shape, 0.05)

    n_pages = PAST_LEN // PAGE_SIZE
    n_used = DEC_BATCH * n_pages
    n_pool = n_used + n_used // 4
    x_prompt = nrm((BATCH, SEQ, D_MODEL))
    x_sample = nrm((DEC_BATCH, DEC_SEQ, D_MODEL))
    p_prompt = nrm((DEPTH, BATCH, SEQ, PLE_DIM))
    p_sample = nrm((DEPTH, DEC_BATCH, DEC_SEQ, PLE_DIM))
    cache_ckv = nrm((DEPTH, n_pool, PAGE_SIZE, KV_RANK))
    cache_kpe = nrm((DEPTH, n_pool, PAGE_SIZE, A_ROPE))
    state_C = nrm((DEPTH, DEC_BATCH, M_HEADS, M_DQK, M_DV))
    state_n = nrm((DEPTH, DEC_BATCH, M_HEADS, M_DQK))
    state_m = nrm((DEPTH, DEC_BATCH, M_HEADS))
    page_table = jax.random.permutation(next(ks), n_pool)[:n_used].reshape(DEC_BATCH, n_pages).astype(jnp.int32)
    b_if = jnp.concatenate([nrm((DEPTH, M_HEADS), 0.1), 3.0 + nrm((DEPTH, M_HEADS), 0.5)], axis=-1)
    return {
        'x_prompt': x_prompt, 'x_sample': x_sample, 'p_prompt': p_prompt, 'p_sample': p_sample,
        'cache_ckv': cache_ckv, 'cache_kpe': cache_kpe,
        'state_C': state_C, 'state_n': state_n, 'state_m': state_m, 'page_table': page_table,
        'g_mix_norm': gain((DEPTH, D_MODEL)),
        'w_in': nrm((DEPTH, D_MODEL, N_IN), D_MODEL ** -0.5),
        'b_if': b_if,
        'g_mh': gain((DEPTH, M_WIDTH)),
        'g_cq': gain((DEPTH, Q_RANK)),
        'w_uq': nrm((DEPTH, Q_RANK, A_HEADS * (A_NOPE + A_ROPE)), Q_RANK ** -0.5),
        'g_ckv': gain((DEPTH, KV_RANK)),
        'w_uk': nrm((DEPTH, KV_RANK, A_HEADS * A_NOPE), KV_RANK ** -0.5),
        'w_uv': nrm((DEPTH, KV_RANK, A_HEADS * A_VD), KV_RANK ** -0.5),
        'g_qn': gain((DEPTH, A_NOPE)), 'g_qr': gain((DEPTH, A_ROPE)),
        'g_kn': gain((DEPTH, A_NOPE)), 'g_kr': gain((DEPTH, A_ROPE)),
        'w_out': nrm((DEPTH, MIX_WIDTH, D_MODEL), MIX_WIDTH ** -0.5),
        'g_ffn_norm': gain((DEPTH, D_MODEL)),
        'w_gate': nrm((N_DENSE, D_MODEL, D_FF), D_MODEL ** -0.5),
        'w_up': nrm((N_DENSE, D_MODEL, D_FF), D_MODEL ** -0.5),
        'w_down': nrm((N_DENSE, D_FF, D_MODEL), D_FF ** -0.5),
        'w_router': nrm((N_MOE, D_MODEL, N_EXPERTS), D_MODEL ** -0.5),
        'w_gate_e': nrm((N_MOE, N_EXPERTS, D_MODEL, D_FF_EXPERT), D_MODEL ** -0.5),
        'w_up_e': nrm((N_MOE, N_EXPERTS, D_MODEL, D_FF_EXPERT), D_MODEL ** -0.5),
        'w_down_e': nrm((N_MOE, N_EXPERTS, D_FF_EXPERT, D_MODEL), D_FF_EXPERT ** -0.5),
        'g_ple_norm': gain((DEPTH, D_MODEL)),
        'w_ple_gate': nrm((DEPTH, D_MODEL, D_MODEL), D_MODEL ** -0.5),
        'w_ple_proj': nrm((DEPTH, PLE_DIM, D_MODEL), PLE_DIM ** -0.5),
    }


def reference(x_prompt, x_sample, p_prompt, p_sample, cache_ckv, cache_kpe, state_C, state_n, state_m,
              page_table, g_mix_norm, w_in, b_if, g_mh, g_cq, w_uq, g_ckv, w_uk, w_uv, g_qn, g_qr, g_kn, g_kr,
              w_out, g_ffn_norm, w_gate, w_up, w_down, w_router, w_gate_e, w_up_e, w_down_e,
              g_ple_norm, w_ple_gate, w_ple_proj):
    f32 = jnp.float32
    n_pages = page_table.shape[1]
    past = n_pages * PAGE_SIZE

    def key_nope(c, l):
        kn = (c @ w_uk[l]).reshape(c.shape[:-1] + (A_HEADS, A_NOPE))
        return rmsnorm(kn, g_kn[l])

    def attend_prompt(l, q_nope, q_pe, c_kv, k_pe):
        b, s = q_nope.shape[:2]
        nb = s // Q_BLOCK
        k_nope = key_nope(c_kv, l)
        wuv = w_uv[l].reshape(KV_RANK, A_HEADS, A_VD)
        kpos = jnp.arange(s)

        def blocks(a):
            return jnp.swapaxes(a.reshape((b, nb, Q_BLOCK) + a.shape[2:]), 0, 1)

        def one_block(args):
            qn, qp, bi = args
            qpos = bi * Q_BLOCK + jnp.arange(Q_BLOCK)
            valid = kpos[None, :] <= qpos[:, None]
            return mla_attend(qn, qp, k_nope, k_pe, c_kv, wuv, valid)

        out = lax.map(one_block, (blocks(q_nope), blocks(q_pe), jnp.arange(nb)))
        return jnp.swapaxes(out, 0, 1).reshape(b, s, A_HEADS, A_VD)

    def attend_sample(l, q_nope, q_pe, c_kv, k_pe):
        s = q_nope.shape[1]
        wuv = w_uv[l].reshape(KV_RANK, A_HEADS, A_VD)
        kpos = jnp.arange(past + s)
        qpos = past + jnp.arange(s)
        valid = kpos[None, :] <= qpos[:, None]

        def one_seq(args):
            qn, qp, cn, rn, pages = args
            c_all = jnp.concatenate([cache_ckv[l, pages].reshape(past, KV_RANK), cn.astype(cache_ckv.dtype)], axis=0)
            r_all = jnp.concatenate([cache_kpe[l, pages].reshape(past, A_ROPE), rn.astype(cache_kpe.dtype)], axis=0)
            k_nope = key_nope(c_all, l)
            return mla_attend(qn[None], qp[None], k_nope[None], r_all[None], c_all[None], wuv, valid)[0]

        return lax.map(one_seq, (q_nope, q_pe, c_kv, k_pe, page_table))

    def token_mixers(x, pos, l, C0, n0, m0, attend):
        b, s = x.shape[:2]
        h = rmsnorm(x, g_mix_norm[l])
        mq, mk, mv, mo, mi, mf, cq, ckv, kpe = split_in(h @ w_in[l])
        i_pre = mi.astype(f32) + b_if[l, :M_HEADS].astype(f32)
        f_pre = mf.astype(f32) + b_if[l, M_HEADS:].astype(f32)
        hm, C1, n1, m1 = mlstm_chunkwise(mq.reshape(b, s, M_HEADS, M_DQK), mk.reshape(b, s, M_HEADS, M_DQK),
                                         mv.reshape(b, s, M_HEADS, M_DV), i_pre, f_pre, C0, n0, m0)
        hm = rmsnorm(hm, g_mh[l].reshape(M_HEADS, M_DV)).reshape(b, s, M_WIDTH)
        hm = (hm * jax.nn.sigmoid(mo.astype(f32))).astype(x.dtype)
        c_q = rmsnorm(cq, g_cq[l])
        qf = (c_q @ w_uq[l]).reshape(b, s, A_HEADS, A_NOPE + A_ROPE)
        q_nope = rmsnorm(qf[..., :A_NOPE], g_qn[l])
        q_pe = rope(rmsnorm(qf[..., A_NOPE:], g_qr[l]), pos)
        c_kv = rmsnorm(ckv, g_ckv[l])
        k_pe = rope(rmsnorm(kpe, g_kr[l]), pos)
        ha = attend(l, q_nope, q_pe, c_kv, k_pe).reshape(b, s, A_WIDTH).astype(x.dtype)
        mix = jnp.concatenate([hm, ha], axis=-1)
        return x + mix @ w_out[l], c_kv, k_pe, C1, n1, m1

    def channel_and_ple(x, p_l, l):
        h = rmsnorm(x, g_ffn_norm[l])
        if l % 2 == 0:
            f = swiglu(h, w_gate[l // 2], w_up[l // 2], w_down[l // 2])
        else:
            f = moe_swiglu(h, w_router[l // 2], w_gate_e[l // 2], w_up_e[l // 2], w_down_e[l // 2])
        x = x + f
        g = jax.nn.sigmoid((rmsnorm(x, g_ple_norm[l]) @ w_ple_gate[l]).astype(f32))
        return x + (g * (p_l @ w_ple_proj[l]).astype(f32)).astype(x.dtype)

    b_p, s_p = x_prompt.shape[:2]
    s_s = x_sample.shape[1]
    pos_p = jnp.arange(s_p)
    pos_s = past + jnp.arange(s_s)
    zC = jnp.zeros((b_p, M_HEADS, M_DQK, M_DV), f32)
    zn = jnp.zeros((b_p, M_HEADS, M_DQK), f32)
    zm = jnp.zeros((b_p, M_HEADS), f32)

    y_p, y_s = x_prompt, x_sample
    ckv_p, kpe_p, C_p, n_p, m_p = [], [], [], [], []
    ckv_s, kpe_s, C_s, n_s, m_s = [], [], [], [], []
    for l in range(DEPTH):
        y_p, c1, r1, C1, n1, m1 = token_mixers(y_p, pos_p, l, zC, zn, zm, attend_prompt)
        y_s, c2, r2, C2, n2, m2 = token_mixers(y_s, pos_s, l, state_C[l], state_n[l], state_m[l], attend_sample)
        y_p = channel_and_ple(y_p, p_prompt[l], l)
        y_s = channel_and_ple(y_s, p_sample[l], l)
        ckv_p.append(c1); kpe_p.append(r1); C_p.append(C1); n_p.append(n1); m_p.append(m1)
        ckv_s.append(c2); kpe_s.append(r2); C_s.append(C2); n_s.append(n2); m_s.append(m2)

    return (y_p, y_s,
            jnp.stack(ckv_p), jnp.stack(kpe_p), jnp.stack(C_p), jnp.stack(n_p), jnp.stack(m_p),
            jnp.stack(ckv_s), jnp.stack(kpe_s), jnp.stack(C_s), jnp.stack(n_s), jnp.stack(m_s))
```

```python
import functools

import jax
import jax.numpy as jnp
from jax import lax
from jax.experimental import pallas as pl
from jax.experimental.pallas import tpu as pltpu

F32 = jnp.float32
BF16 = jnp.bfloat16

M_HEADS = 4
M_DQK = 128
M_DV = 256
M_WIDTH = M_HEADS * M_DV
M_CHUNK = 64
A_HEADS = 8
A_NOPE = 128
A_ROPE = 64
A_VD = 128
A_WIDTH = A_HEADS * A_VD
Q_RANK = 512
KV_RANK = 256
ROPE_THETA = 10000.0
MLA_SCALE = (A_NOPE + A_ROPE) ** -0.5
PAGE_SIZE = 128
N_EXPERTS = 8
EPS = 1e-6

LANE = 128
MIB = 2 ** 20

COL_MQ = 0
COL_MK = COL_MQ + M_HEADS * M_DQK
COL_MV = COL_MK + M_HEADS * M_DQK
COL_MO = COL_MV + M_WIDTH
COL_CQ = COL_MO + M_WIDTH
COL_CKV = COL_CQ + Q_RANK
COL_KG = COL_CKV + KV_RANK
N_IN_PAD = 4096
GATE_I = A_ROPE
GATE_F = A_ROPE + M_HEADS
QK_HEAD = 2 * LANE
NEG_BIG = -1e30


def _cparams(sem, vmem_mib):
    return pltpu.CompilerParams(dimension_semantics=sem, vmem_limit_bytes=vmem_mib * MIB)


def _rms(x, g, n):
    ms = jnp.sum(x * x, axis=-1, keepdims=True) * (1.0 / n)
    return x * lax.rsqrt(ms + EPS) * g


def _dot(a, b):
    return jnp.dot(a, b, preferred_element_type=F32)


def _dot_nt(a, b):
    return lax.dot_general(a, b, (((1,), (1,)), ((), ())), preferred_element_type=F32)


def _dot_tn(a, b):
    return lax.dot_general(a, b, (((0,), (0,)), ((), ())), preferred_element_type=F32)


def _norm_matmul_kernel(x_ref, g_ref, w_ref, o_ref, xn_ref):
    @pl.when(pl.program_id(1) == 0)
    def _():
        xn_ref[...] = _rms(x_ref[...], g_ref[...], x_ref.shape[-1]).astype(BF16)

    o_ref[...] = _dot(xn_ref[...], w_ref[...])


def norm_matmul(x, g, w, tm, tn):
    t, d = x.shape
    n = w.shape[1]
    return pl.pallas_call(
        _norm_matmul_kernel,
        grid=(t // tm, n // tn),
        in_specs=[pl.BlockSpec((tm, d), lambda i, j: (i, 0)),
                  pl.BlockSpec((1, d), lambda i, j: (0, 0)),
                  pl.BlockSpec((d, tn), lambda i, j: (0, j))],
        out_specs=pl.BlockSpec((tm, tn), lambda i, j: (i, j)),
        out_shape=jax.ShapeDtypeStruct((t, n), F32),
        scratch_shapes=[pltpu.VMEM((tm, d), BF16)],
        compiler_params=_cparams(("parallel", "arbitrary"), 40),
        name="norm_matmul",
    )(x, g, w)


def _mla_prep_kernel(cq_ref, ckv_ref, kg_ref, cos_ref, sin_ref, wuq_ref, wuk_ref, wuv_ref,
                     gcq_ref, gckv_ref, gkr_ref, gqn_ref, gqr_ref, gkn_ref,
                     ckv_out, kpe_out, q_out, k_out, v_out, *a_out, absorb):
    cos = cos_ref[...]
    sin = sin_ref[...]

    def rope(x):
        return x * cos + (pltpu.roll(x, A_ROPE // 2, 1) + pltpu.roll(x, LANE - A_ROPE // 2, 1)) * sin

    ckv = _rms(ckv_ref[...], gckv_ref[...], KV_RANK)
    ckv_out[...] = ckv
    ckv_b = ckv.astype(BF16)

    kg = kg_ref[...]
    lane = lax.broadcasted_iota(jnp.int32, kg.shape, 1)
    kp = jnp.where(lane < A_ROPE, kg, 0.0)
    kpr = rope(_rms(kp, gkr_ref[...], A_ROPE))
    kpe_out[...] = kpr[:, :A_ROPE]
    kpr_b = kpr.astype(BF16)

    kn = _dot(ckv_b, wuk_ref[...])
    v_out[...] = _dot(ckv_b, wuv_ref[...]).astype(BF16)
    cq = _rms(cq_ref[...], gcq_ref[...], Q_RANK).astype(BF16)
    qf = _dot(cq, wuq_ref[...])
    gkn = gkn_ref[...]
    for h in range(A_HEADS):
        n0 = h * A_NOPE
        c0 = h * QK_HEAD
        k_out[:, c0:c0 + LANE] = _rms(kn[:, n0:n0 + A_NOPE], gkn, A_NOPE).astype(BF16)
        k_out[:, c0 + LANE:c0 + QK_HEAD] = kpr_b
        qn = _rms(qf[:, c0:c0 + LANE], gqn_ref[...], A_NOPE)
        q_out[:, c0:c0 + LANE] = qn.astype(BF16)
        qp = _rms(qf[:, c0 + LANE:c0 + QK_HEAD], gqr_ref[...], A_ROPE)
        q_out[:, c0 + LANE:c0 + QK_HEAD] = rope(qp).astype(BF16)
        if absorb:
            a_out[0][:, h * KV_RANK:(h + 1) * KV_RANK] = _dot_nt(
                (qn * gkn).astype(BF16), wuk_ref[:, n0:n0 + A_NOPE])


def mla_prep(z, cos, sin, lw, tm, n_pos_blocks, absorb):
    t = z.shape[0]
    row = lambda i: (i, 0)
    full = lambda i: (0, 0)
    out_shape = [jax.ShapeDtypeStruct((t, KV_RANK), F32),
                 jax.ShapeDtypeStruct((t, A_ROPE), F32),
                 jax.ShapeDtypeStruct((t, A_HEADS * QK_HEAD), BF16),
                 jax.ShapeDtypeStruct((t, A_HEADS * QK_HEAD), BF16),
                 jax.ShapeDtypeStruct((t, A_WIDTH), BF16)]
    out_specs = [pl.BlockSpec((tm, KV_RANK), row), pl.BlockSpec((tm, A_ROPE), row),
                 pl.BlockSpec((tm, A_HEADS * QK_HEAD), row), pl.BlockSpec((tm, A_HEADS * QK_HEAD), row),
                 pl.BlockSpec((tm, A_WIDTH), row)]
    if absorb:
        out_shape.append(jax.ShapeDtypeStruct((t, A_HEADS * KV_RANK), F32))
        out_specs.append(pl.BlockSpec((tm, A_HEADS * KV_RANK), row))
    return pl.pallas_call(
        functools.partial(_mla_prep_kernel, absorb=absorb),
        grid=(t // tm,),
        in_specs=[pl.BlockSpec((tm, Q_RANK), lambda i: (i, COL_CQ // Q_RANK)),
                  pl.BlockSpec((tm, KV_RANK), lambda i: (i, COL_CKV // KV_RANK)),
                  pl.BlockSpec((tm, LANE), lambda i: (i, COL_KG // LANE)),
                  pl.BlockSpec((tm, LANE), lambda i: (i % n_pos_blocks, 0)),
                  pl.BlockSpec((tm, LANE), lambda i: (i % n_pos_blocks, 0)),
                  pl.BlockSpec((Q_RANK, A_HEADS * QK_HEAD), full),
                  pl.BlockSpec((KV_RANK, A_HEADS * A_NOPE), full),
                  pl.BlockSpec((KV_RANK, A_WIDTH), full),
                  pl.BlockSpec((1, Q_RANK), full), pl.BlockSpec((1, KV_RANK), full),
                  pl.BlockSpec((1, LANE), full), pl.BlockSpec((1, LANE), full),
                  pl.BlockSpec((1, LANE), full), pl.BlockSpec((1, LANE), full)],
        out_specs=out_specs,
        out_shape=out_shape,
        compiler_params=_cparams(("parallel",), 48),
        name="mla_prep",
    )(z, z, z, cos, sin, lw["w_uq"], lw["w_uk"], lw["w_uv"], lw["g_cq"], lw["g_ckv"],
      lw["g_kr"], lw["g_qn"], lw["g_qr"], lw["g_kn"])


def _mlstm_kernel(q_ref, k_ref, v_ref, o_ref, kg_ref, bias_ref, gmh_ref, c0_ref, n0_ref, m0_ref,
                  h_out, c_out, n_out, m_out, c_s, n_s, m_s, *, chunk, n_real):
    c = pl.program_id(1)

    @pl.when(c == 0)
    def _():
        c_s[...] = c0_ref[...]
        n_s[...] = n0_ref[...]
        m_s[...] = m0_ref[...]

    L = chunk
    gates = kg_ref[...] + bias_ref[...]
    lf_all = jax.nn.log_sigmoid(gates)
    ri = lax.broadcasted_iota(jnp.int32, (L, L), 0)
    ci = lax.broadcasted_iota(jnp.int32, (L, L), 1)
    tri = ci <= ri
    eye = ci == ri
    rowid = lax.broadcasted_iota(jnp.int32, (L, 1), 0)
    real = rowid >= (L - n_real)
    scale = M_DQK ** -0.5

    def to_row(col):
        return jnp.sum(jnp.where(eye, col, 0.0), axis=0, keepdims=True)

    for h in range(M_HEADS):
        i_col = gates[:, GATE_I + h:GATE_I + h + 1]
        lf_col = lf_all[:, GATE_F + h:GATE_F + h + 1]
        if n_real < L:
            i_col = jnp.where(real, i_col, NEG_BIG)
            lf_col = jnp.where(real, lf_col, 0.0)
        i_row = to_row(i_col)
        lf_row = to_row(lf_col)
        bh_col = jnp.sum(jnp.where(tri, lf_row, 0.0), axis=1, keepdims=True)
        bh_row = to_row(bh_col)
        dmat = jnp.where(tri, bh_col - bh_row + i_row, -jnp.inf)
        m_prev = m_s[h][:, 0:1]
        inter = bh_col + m_prev
        m_t = jnp.maximum(inter, jnp.max(dmat, axis=1, keepdims=True))
        w_inter = jnp.exp(inter - m_t)
        qh = q_ref[:, h * M_DQK:(h + 1) * M_DQK]
        kh = k_ref[:, h * M_DQK:(h + 1) * M_DQK]
        vh = v_ref[:, h * M_DV:(h + 1) * M_DV]
        qb = qh.astype(BF16)
        vb = vh.astype(BF16)
        s_qk = _dot_nt(qb, kh.astype(BF16)) * scale * jnp.exp(dmat - m_t)
        c_prev = c_s[h]
        n_prev = n_s[h]
        num = _dot(s_qk.astype(BF16), vb) + _dot(qb, c_prev.astype(BF16)) * (scale * w_inter)
        qn = jnp.sum(qh * n_prev, axis=1, keepdims=True) * scale
        den = jnp.sum(s_qk, axis=1, keepdims=True) + w_inter * qn
        den = jnp.maximum(jnp.abs(den), jnp.exp(-m_t))
        hh = num / den
        m_last = m_t[L - 1:L, :]
        bh_last = bh_col[L - 1:L, :]
        g_inter = jnp.exp(bh_last + m_prev - m_last)
        g_intra = jnp.exp(bh_last - bh_col + i_col - m_last)
        kw = kh * g_intra
        c_s[h] = g_inter * c_prev + _dot_tn(kw.astype(BF16), vb)
        n_s[h] = g_inter * n_prev + jnp.sum(kw, axis=0, keepdims=True)
        m_s[h] = jnp.broadcast_to(m_last, (1, LANE))
        hn = _rms(hh, gmh_ref[:, h * M_DV:(h + 1) * M_DV], M_DV)
        og = o_ref[:, h * M_DV:(h + 1) * M_DV]
        h_out[:, h * M_DV:(h + 1) * M_DV] = (hn * jax.nn.sigmoid(og)).astype(BF16)

    @pl.when(c == pl.num_programs(1) - 1)
    def _():
        c_out[...] = c_s[...]
        n_out[...] = n_s[...]
        m_out[...] = m_s[...]


def mlstm(z, bias, gmh, c0, n0, m0, batch, n_chunks, chunk, n_real):
    t = z.shape[0]
    blk = lambda w, col: pl.BlockSpec((chunk, w), lambda b, c: (b * n_chunks + c, col))
    st = lambda a: pl.BlockSpec((None,) + a.shape[1:], lambda b, c: (b, 0, 0, 0))
    full = lambda b, c: (0, 0)
    return pl.pallas_call(
        functools.partial(_mlstm_kernel, chunk=chunk, n_real=n_real),
        grid=(batch, n_chunks),
        in_specs=[blk(M_HEADS * M_DQK, COL_MQ // (M_HEADS * M_DQK)),
                  blk(M_HEADS * M_DQK, COL_MK // (M_HEADS * M_DQK)),
                  blk(M_WIDTH, COL_MV // M_WIDTH), blk(M_WIDTH, COL_MO // M_WIDTH),
                  blk(LANE, COL_KG // LANE),
                  pl.BlockSpec((1, LANE), full), pl.BlockSpec((1, M_WIDTH), full),
                  st(c0), st(n0), st(m0)],
        out_specs=[pl.BlockSpec((chunk, M_WIDTH), lambda b, c: (b * n_chunks + c, 0)),
                   st(c0), st(n0), st(m0)],
        out_shape=[jax.ShapeDtypeStruct((t, M_WIDTH), BF16),
                   jax.ShapeDtypeStruct(c0.shape, F32), jax.ShapeDtypeStruct(n0.shape, F32),
                   jax.ShapeDtypeStruct(m0.shape, F32)],
        scratch_shapes=[pltpu.VMEM(c0.shape[1:], F32), pltpu.VMEM(n0.shape[1:], F32),
                        pltpu.VMEM(m0.shape[1:], F32)],
        compiler_params=_cparams(("parallel", "arbitrary"), 32),
        name="mlstm",
    )(z, z, z, z, z, bias, gmh, c0, n0, m0)


def _attn_kernel(q_ref, k_ref, v_ref, o_ref, *, tq):
    qi = pl.program_id(2)
    q = q_ref[...]
    row = lax.broadcasted_iota(jnp.int32, (tq, tq), 0) + qi * tq
    col0 = lax.broadcasted_iota(jnp.int32, (tq, tq), 1)

    def body(j, carry):
        m, l, acc = carry
        start = pl.multiple_of(j * tq, tq)
        kb = k_ref[pl.ds(start, tq), :]
        vb = v_ref[pl.ds(start, tq), :]
        s = _dot_nt(q, kb) * MLA_SCALE
        s = jnp.where(col0 + j * tq <= row, s, -jnp.inf)
        m_new = jnp.maximum(m, jnp.max(s, axis=1, keepdims=True))
        alpha = jnp.exp(m - m_new)
        p = jnp.exp(s - m_new)
        l = alpha * l + jnp.sum(p, axis=1, keepdims=True)
        acc = alpha * acc + _dot(p.astype(BF16), vb)
        return m_new, l, acc

    init = (jnp.full((tq, 1), -jnp.inf, F32), jnp.zeros((tq, 1), F32), jnp.zeros((tq, A_VD), F32))
    _, l, acc = lax.fori_loop(0, qi + 1, body, init)
    o_ref[...] = (acc / l).astype(o_ref.dtype)


def prompt_attention(q, k, v, batch, seq, tq):
    nq = seq // tq
    return pl.pallas_call(
        functools.partial(_attn_kernel, tq=tq),
        grid=(batch, A_HEADS, nq),
        in_specs=[pl.BlockSpec((tq, QK_HEAD), lambda b, h, i: (b * nq + i, h)),
                  pl.BlockSpec((seq, QK_HEAD), lambda b, h, i: (b, h)),
                  pl.BlockSpec((seq, A_VD), lambda b, h, i: (b, h))],
        out_specs=pl.BlockSpec((tq, A_VD), lambda b, h, i: (b * nq + i, h)),
        out_shape=jax.ShapeDtypeStruct((batch * seq, A_WIDTH), BF16),
        compiler_params=_cparams(("parallel", "parallel", "arbitrary"), 32),
        name="prompt_attention",
    )(q, k, v)


def _decode_attn_kernel(pt_ref, *refs, pps):
    del pt_ref
    ckv_refs = refs[:pps]
    kpe_refs = refs[pps:2 * pps]
    (q_ref, knew_ref, a_ref, cnew_ref, wukt_ref, wuv_ref, o_ref,
     lhs_s, cb_s, kb_s, m_s, l_s, acc_s) = refs[2 * pps:]
    c = pl.program_id(1)
    n_up = A_HEADS * A_NOPE

    @pl.when(c == 0)
    def _():
        lhs_s[0:n_up, :] = wukt_ref[...]
        lhs_s[n_up:, :] = jnp.concatenate(
            [a_ref[...], jnp.zeros((lhs_s.shape[0] - n_up - A_HEADS, KV_RANK), F32)], axis=0).astype(BF16)
        s_new = jnp.sum(q_ref[...].astype(F32) * knew_ref[...].astype(F32), axis=1, keepdims=True) * MLA_SCALE
        m_s[...] = jnp.broadcast_to(s_new, m_s.shape)
        l_s[...] = jnp.ones(l_s.shape, F32)
        acc_s[...] = jnp.broadcast_to(cnew_ref[...], acc_s.shape)

    for p in range(pps):
        cb_s[p * PAGE_SIZE:(p + 1) * PAGE_SIZE, :] = ckv_refs[p][...].astype(BF16)
        kb_s[p * PAGE_SIZE:(p + 1) * PAGE_SIZE, :] = kpe_refs[p][...].astype(BF16)
    cb = cb_s[...]
    tc = cb.shape[0]
    up = _dot_nt(lhs_s[...], cb)
    hid = lax.broadcasted_iota(jnp.int32, (A_HEADS, tc), 0)
    ss = jnp.zeros((A_HEADS, tc), F32)
    for h in range(A_HEADS):
        blk = up[h * A_NOPE:(h + 1) * A_NOPE, :]
        ss = jnp.where(hid == h, jnp.sum(blk * blk, axis=0, keepdims=True), ss)
    s_nope = up[n_up:n_up + A_HEADS, :] * lax.rsqrt(ss * (1.0 / A_NOPE) + EPS)
    s_pe = _dot_nt(q_ref[:, LANE:LANE + A_ROPE], kb_s[...])
    s = (s_nope + s_pe) * MLA_SCALE
    m_old = m_s[:, 0:1]
    m_new = jnp.maximum(m_old, jnp.max(s, axis=1, keepdims=True))
    alpha = jnp.exp(m_old - m_new)
    p = jnp.exp(s - m_new)
    l_s[...] = alpha * l_s[...] + jnp.sum(p, axis=1, keepdims=True)
    acc_s[...] = alpha * acc_s[...] + _dot(p.astype(BF16), cb)
    m_s[...] = jnp.broadcast_to(m_new, m_s.shape)

    @pl.when(c == pl.num_programs(1) - 1)
    def _():
        o_lat = acc_s[...] / l_s[:, 0:1]
        full = _dot(o_lat.astype(BF16), wuv_ref[...])
        hid_o = lax.broadcasted_iota(jnp.int32, (A_HEADS, A_VD), 0)
        res = jnp.zeros((A_HEADS, A_VD), F32)
        for h in range(A_HEADS):
            res = jnp.where(hid_o == h, full[:, h * A_VD:(h + 1) * A_VD], res)
        o_ref[...] = res


def decode_attention(layer, page_table, cache_ckv, cache_kpe, q3, k3, a3, cnew3, wukt, wuv, pps):
    nb, n_pages = page_table.shape
    n_up = A_HEADS * A_NOPE
    tc = pps * PAGE_SIZE

    def page_spec(width, k):
        return pl.BlockSpec((None, None, PAGE_SIZE, width),
                            lambda b, c, pt: (layer, pt[b, c * pps + k], 0, 0))

    per_seq = lambda rows, w: pl.BlockSpec((None, rows, w), lambda b, c, pt: (b, 0, 0))
    full = lambda b, c, pt: (0, 0)
    in_specs = ([page_spec(KV_RANK, k) for k in range(pps)] + [page_spec(A_ROPE, k) for k in range(pps)]
                + [per_seq(A_HEADS, QK_HEAD), per_seq(A_HEADS, QK_HEAD), per_seq(A_HEADS, KV_RANK),
                   per_seq(1, KV_RANK),
                   pl.BlockSpec((n_up, KV_RANK), full), pl.BlockSpec((KV_RANK, A_WIDTH), full)])
    return pl.pallas_call(
        functools.partial(_decode_attn_kernel, pps=pps),
        grid_spec=pltpu.PrefetchScalarGridSpec(
            num_scalar_prefetch=1,
            grid=(nb, n_pages // pps),
            in_specs=in_specs,
            out_specs=pl.BlockSpec((None, A_HEADS, A_VD), lambda b, c, pt: (b, 0, 0)),
            scratch_shapes=[pltpu.VMEM((n_up + 16, KV_RANK), BF16), pltpu.VMEM((tc, KV_RANK), BF16),
                            pltpu.VMEM((tc, A_ROPE), BF16), pltpu.VMEM((A_HEADS, LANE), F32),
                            pltpu.VMEM((A_HEADS, LANE), F32), pltpu.VMEM((A_HEADS, KV_RANK), F32)]),
        out_shape=jax.ShapeDtypeStruct((nb, A_HEADS, A_VD), F32),
        compiler_params=_cparams(("parallel", "arbitrary"), 40),
        name="decode_attention",
    )(page_table, *([cache_ckv] * pps), *([cache_kpe] * pps), q3, k3, a3, cnew3, wukt, wuv)


def _out_proj_kernel(hm_ref, ha_ref, w_ref, x_ref, o_ref):
    acc = _dot(hm_ref[...], w_ref[0:M_WIDTH, :]) + _dot(ha_ref[...], w_ref[M_WIDTH:, :])
    o_ref[...] = x_ref[...] + acc


def out_proj(hm, ha, w, x, tm, tn):
    t, d = x.shape
    return pl.pallas_call(
        _out_proj_kernel,
        grid=(t // tm, d // tn),
        in_specs=[pl.BlockSpec((tm, M_WIDTH), lambda i, j: (i, 0)),
                  pl.BlockSpec((tm, A_WIDTH), lambda i, j: (i, 0)),
                  pl.BlockSpec((M_WIDTH + A_WIDTH, tn), lambda i, j: (0, j)),
                  pl.BlockSpec((tm, tn), lambda i, j: (i, j))],
        out_specs=pl.BlockSpec((tm, tn), lambda i, j: (i, j)),
        out_shape=jax.ShapeDtypeStruct((t, d), F32),
        compiler_params=_cparams(("parallel", "arbitrary"), 40),
        name="out_proj",
    )(hm, ha, w, x)


def _ffn_kernel(x_ref, g_ref, wg_ref, wu_ref, wd_ref, o_ref, xn_ref, acc_ref):
    f = pl.program_id(1)

    @pl.when(f == 0)
    def _():
        xn_ref[...] = _rms(x_ref[...], g_ref[...], x_ref.shape[-1]).astype(BF16)
        acc_ref[...] = jnp.zeros(acc_ref.shape, F32)

    xn = xn_ref[...]
    a = jax.nn.silu(_dot(xn, wg_ref[...])) * _dot(xn, wu_ref[...])
    acc_ref[...] += _dot(a.astype(BF16), wd_ref[...])

    @pl.when(f == pl.num_programs(1) - 1)
    def _():
        o_ref[...] = x_ref[...] + acc_ref[...]


def ffn_dense(x, g, wg, wu, wd, tm, tf):
    t, d = x.shape
    ff = wg.shape[1]
    return pl.pallas_call(
        _ffn_kernel,
        grid=(t // tm, ff // tf),
        in_specs=[pl.BlockSpec((tm, d), lambda i, f: (i, 0)),
                  pl.BlockSpec((1, d), lambda i, f: (0, 0)),
                  pl.BlockSpec((d, tf), lambda i, f: (0, f)),
                  pl.BlockSpec((d, tf), lambda i, f: (0, f)),
                  pl.BlockSpec((tf, d), lambda i, f: (f, 0))],
        out_specs=pl.BlockSpec((tm, d), lambda i, f: (i, 0)),
        out_shape=jax.ShapeDtypeStruct((t, d), F32),
        scratch_shapes=[pltpu.VMEM((tm, d), BF16), pltpu.VMEM((tm, d), F32)],
        compiler_params=_cparams(("parallel", "arbitrary"), 48),
        name="ffn_dense",
    )(x, g, wg, wu, wd)


def _ple_kernel(x_ref, g_ref, p_ref, wg_ref, wp_ref, o_ref, xn_ref, *, tn):
    j = pl.program_id(1)

    @pl.when(j == 0)
    def _():
        xn_ref[...] = _rms(x_ref[...], g_ref[...], x_ref.shape[-1]).astype(BF16)

    gate = jax.nn.sigmoid(_dot(xn_ref[...], wg_ref[...]))
    proj = _dot(p_ref[...].astype(BF16), wp_ref[...])
    o_ref[...] = x_ref[:, pl.ds(pl.multiple_of(j * tn, tn), tn)] + gate * proj


def ple(x, g, p, wg, wp, tm, tn):
    t, d = x.shape
    pd = p.shape[1]
    return pl.pallas_call(
        functools.partial(_ple_kernel, tn=tn),
        grid=(t // tm, d // tn),
        in_specs=[pl.BlockSpec((tm, d), lambda i, j: (i, 0)),
                  pl.BlockSpec((1, d), lambda i, j: (0, 0)),
                  pl.BlockSpec((tm, pd), lambda i, j: (i, 0)),
                  pl.BlockSpec((d, tn), lambda i, j: (0, j)),
                  pl.BlockSpec((pd, tn), lambda i, j: (0, j))],
        out_specs=pl.BlockSpec((tm, tn), lambda i, j: (i, j)),
        out_shape=jax.ShapeDtypeStruct((t, d), F32),
        scratch_shapes=[pltpu.VMEM((tm, d), BF16)],
        compiler_params=_cparams(("parallel", "arbitrary"), 40),
        name="ple",
    )(x, g, p, wg, wp)


def _router_kernel(x_ref, g_ref, wr_ref, idx_out, gate_out, cnt_out, carry_s):
    i = pl.program_id(0)

    @pl.when(i == 0)
    def _():
        carry_s[...] = jnp.zeros(carry_s.shape, F32)

    hn = _rms(x_ref[...], g_ref[...], x_ref.shape[-1])
    logits = jnp.dot(hn, wr_ref[...], preferred_element_type=F32, precision=lax.Precision.HIGHEST)
    tm = logits.shape[0]
    lane = lax.broadcasted_iota(jnp.int32, logits.shape, 1)
    lg = jnp.where(lane < N_EXPERTS, logits, -jnp.inf)
    m1 = jnp.max(lg, axis=1, keepdims=True)
    i1 = jnp.min(jnp.where(lg == m1, lane, LANE), axis=1, keepdims=True)
    lg2 = jnp.where(lane == i1, -jnp.inf, lg)
    m2 = jnp.max(lg2, axis=1, keepdims=True)
    i2 = jnp.min(jnp.where(lg2 == m2, lane, LANE), axis=1, keepdims=True)
    e2 = jnp.exp(m2 - m1)
    g1 = 1.0 / (1.0 + e2)
    g2 = e2 / (1.0 + e2)
    hit1 = lane == i1
    hit2 = lane == i2
    onehot = jnp.where(hit1 | hit2, 1.0, 0.0)
    ri = lax.broadcasted_iota(jnp.int32, (tm, tm), 0)
    ci = lax.broadcasted_iota(jnp.int32, (tm, tm), 1)
    before = jnp.where(ci < ri, 1.0, 0.0).astype(BF16)
    seen = carry_s[...] + _dot(before, onehot.astype(BF16))
    r1 = jnp.sum(jnp.where(hit1, seen, 0.0), axis=1, keepdims=True)
    r2 = jnp.sum(jnp.where(hit2, seen, 0.0), axis=1, keepdims=True)
    carry_s[...] += jnp.sum(onehot, axis=0, keepdims=True)
    idx = jnp.where(lane == 0, i1, jnp.where(lane == 1, i2, 0))
    rank = jnp.where(lane == 2, r1, jnp.where(lane == 3, r2, 0.0)).astype(jnp.int32)
    idx_out[...] = idx + rank
    gate_out[...] = jnp.where(lane == 0, g1, jnp.where(lane == 1, g2, 0.0))
    cnt_out[...] = carry_s[...]


def moe_router(x, g, wr, tm):
    t, d = x.shape
    row = lambda i: (i, 0)
    full = lambda i: (0, 0)
    return pl.pallas_call(
        _router_kernel,
        grid=(t // tm,),
        in_specs=[pl.BlockSpec((tm, d), row), pl.BlockSpec((1, d), full), pl.BlockSpec((d, LANE), full)],
        out_specs=[pl.BlockSpec((tm, LANE), row), pl.BlockSpec((tm, LANE), row),
                   pl.BlockSpec((1, LANE), full)],
        out_shape=[jax.ShapeDtypeStruct((t, LANE), jnp.int32),
                   jax.ShapeDtypeStruct((t, LANE), F32), jax.ShapeDtypeStruct((1, LANE), F32)],
        scratch_shapes=[pltpu.VMEM((1, LANE), F32)],
        compiler_params=_cparams(("arbitrary",), 40),
        name="moe_router",
    )(x, g, wr)


def _dispatch_kernel(dest_ref, x_ref, init_ref, xs_ref, sem, *, tm):
    del init_ref
    i = pl.program_id(0)

    def row_copy(r, k):
        t = i * tm + r
        slot = dest_ref[2 * t + k]
        return pltpu.make_async_copy(x_ref.at[pl.ds(t, 1), :], xs_ref.at[pl.ds(slot, 1), :], sem)

    def start(r, carry):
        row_copy(r, 0).start()
        row_copy(r, 1).start()
        return carry

    def wait(r, carry):
        row_copy(r, 0).wait()
        row_copy(r, 1).wait()
        return carry

    lax.fori_loop(0, tm, start, 0)
    lax.fori_loop(0, tm, wait, 0)


def moe_dispatch(dest_flat, x, n_slots, tm):
    t, d = x.shape
    init = jnp.zeros((n_slots, d), F32)
    return pl.pallas_call(
        functools.partial(_dispatch_kernel, tm=tm),
        grid_spec=pltpu.PrefetchScalarGridSpec(
            num_scalar_prefetch=1,
            grid=(t // tm,),
            in_specs=[pl.BlockSpec(memory_space=pl.ANY), pl.BlockSpec(memory_space=pl.ANY)],
            out_specs=pl.BlockSpec(memory_space=pl.ANY),
            scratch_shapes=[pltpu.SemaphoreType.DMA(())]),
        out_shape=jax.ShapeDtypeStruct((n_slots, d), F32),
        input_output_aliases={2: 0},
        compiler_params=_cparams(("arbitrary",), 32),
        name="moe_dispatch",
    )(dest_flat, x, init)


def _expert_ffn_kernel(be_ref, nu_ref, x_ref, g_ref, wg_ref, wu_ref, wd_ref, o_ref, xn_ref, acc_ref):
    del be_ref
    i = pl.program_id(0)
    f = pl.program_id(1)
    used = i < nu_ref[0]

    @pl.when(f == 0)
    def _():
        xn_ref[...] = _rms(x_ref[...], g_ref[...], x_ref.shape[-1]).astype(BF16)
        acc_ref[...] = jnp.zeros(acc_ref.shape, F32)

    @pl.when(used)
    def _():
        x = xn_ref[...]
        a = jax.nn.silu(_dot(x, wg_ref[...])) * _dot(x, wu_ref[...])
        acc_ref[...] += _dot(a.astype(BF16), wd_ref[...])

    @pl.when(f == pl.num_programs(1) - 1)
    def _():
        o_ref[...] = acc_ref[...]


def expert_ffn(blk_e, n_used, xs, g, wg, wu, wd, blk, tf):
    p, d = xs.shape
    ff = wg.shape[2]
    nf = ff // tf

    def wcol(i, f, be, nu):
        live = i < nu[0]
        return (be[i], 0, jnp.where(live, f, nf - 1))

    def wrow(i, f, be, nu):
        live = i < nu[0]
        return (be[i], jnp.where(live, f, nf - 1), 0)

    return pl.pallas_call(
        _expert_ffn_kernel,
        grid_spec=pltpu.PrefetchScalarGridSpec(
            num_scalar_prefetch=2,
            grid=(p // blk, nf),
            in_specs=[pl.BlockSpec((blk, d), lambda i, f, be, nu: (i, 0)),
                      pl.BlockSpec((1, d), lambda i, f, be, nu: (0, 0)),
                      pl.BlockSpec((None, d, tf), wcol), pl.BlockSpec((None, d, tf), wcol),
                      pl.BlockSpec((None, tf, d), wrow)],
            out_specs=pl.BlockSpec((blk, d), lambda i, f, be, nu: (i, 0)),
            scratch_shapes=[pltpu.VMEM((blk, d), BF16), pltpu.VMEM((blk, d), F32)]),
        out_shape=jax.ShapeDtypeStruct((p, d), F32),
        compiler_params=_cparams(("arbitrary", "arbitrary"), 48),
        name="expert_ffn",
    )(blk_e, n_used, xs, g, wg, wu, wd)


def _combine_kernel(dest_ref, x_ref, gate_ref, ys_ref, o_ref, buf, sem):
    i = pl.program_id(0)
    n = pl.num_programs(0)
    tm = x_ref.shape[0]

    def row_copy(step, slot, r, k):
        src = dest_ref[2 * (step * tm + r) + k]
        return pltpu.make_async_copy(ys_ref.at[pl.ds(src, 1), :], buf.at[slot, k, pl.ds(r, 1), :], sem.at[slot])

    def start_block(step, slot):
        def body(r, carry):
            row_copy(step, slot, r, 0).start()
            row_copy(step, slot, r, 1).start()
            return carry
        lax.fori_loop(0, tm, body, 0)

    @pl.when(i == 0)
    def _():
        start_block(0, 0)

    @pl.when(i + 1 < n)
    def _():
        start_block(i + 1, (i + 1) % 2)

    slot = i % 2

    def wait_body(r, carry):
        row_copy(i, slot, r, 0).wait()
        row_copy(i, slot, r, 1).wait()
        return carry

    lax.fori_loop(0, tm, wait_body, 0)
    g = gate_ref[...]
    o_ref[...] = x_ref[...] + (buf[slot, 0] * g[:, 0:1] + buf[slot, 1] * g[:, 1:2])


def moe_combine(dest_flat, x, gates, ys, tm):
    t, d = x.shape
    return pl.pallas_call(
        _combine_kernel,
        grid_spec=pltpu.PrefetchScalarGridSpec(
            num_scalar_prefetch=1,
            grid=(t // tm,),
            in_specs=[pl.BlockSpec((tm, d), lambda i, dr: (i, 0)),
                      pl.BlockSpec((tm, LANE), lambda i, dr: (i, 0)),
                      pl.BlockSpec(memory_space=pl.ANY)],
            out_specs=pl.BlockSpec((tm, d), lambda i, dr: (i, 0)),
            scratch_shapes=[pltpu.VMEM((2, 2, tm, d), F32), pltpu.SemaphoreType.DMA((2,))]),
        out_shape=jax.ShapeDtypeStruct((t, d), F32),
        compiler_params=_cparams(("arbitrary",), 48),
        name="moe_combine",
    )(dest_flat, x, gates, ys)


def moe_mixer(x, g, wr, wg, wu, wd, tm, blk, tf):
    t, d = x.shape
    idx, gates, counts = moe_router(x, g, wr, tm)
    experts = idx[:, 0:2]
    ranks = idx[:, 2:4]
    cnt = counts[0, :N_EXPERTS].astype(jnp.int32)
    padded = (cnt + blk - 1) // blk * blk
    pend = jnp.cumsum(padded)
    pstart = pend - padded
    dest = (pstart[experts] + ranks).reshape(-1).astype(jnp.int32)
    n_blocks = -(-(2 * t) // blk) + N_EXPERTS
    n_used = (pend[-1] // blk).astype(jnp.int32)
    blk_ids = jnp.minimum(jnp.arange(n_blocks, dtype=jnp.int32), n_used - 1)
    blk_e = jnp.minimum(jnp.searchsorted(pend, blk_ids * blk, side="right"), N_EXPERTS - 1).astype(jnp.int32)
    xs = moe_dispatch(dest, x, n_blocks * blk, tm)
    ys = expert_ffn(blk_e, n_used.reshape(1), xs, g, wg, wu, wd, blk, tf)
    return moe_combine(dest, x, gates, ys, tm)


def _rope_tables(pos):
    half = A_ROPE // 2
    inv = ROPE_THETA ** (-jnp.arange(half, dtype=F32) / half)
    ang = pos.astype(F32)[:, None] * inv[None, :]
    cos, sin = jnp.cos(ang), jnp.sin(ang)
    zeros = jnp.zeros((pos.shape[0], LANE - A_ROPE), F32)
    return (jnp.concatenate([cos, cos, zeros], axis=-1), jnp.concatenate([-sin, sin, zeros], axis=-1))


def _pad_lanes(v, width=LANE):
    return jnp.pad(v, (0, width - v.shape[0])).reshape(1, width)


def _layer_weights(l, g_mix_norm, w_in, b_if, g_mh, g_cq, w_uq, g_ckv, w_uk, w_uv, g_qn, g_qr, g_kn, g_kr,
                   w_out, g_ffn_norm, g_ple_norm, w_ple_gate, w_ple_proj):
    d = w_in.shape[1]
    wi = w_in[l]
    o = 0
    parts = {}
    for name, width in (("mq", M_HEADS * M_DQK), ("mk", M_HEADS * M_DQK), ("mv", M_WIDTH), ("mo", M_WIDTH),
                        ("mi", M_HEADS), ("mf", M_HEADS), ("cq", Q_RANK), ("ckv", KV_RANK), ("kpe", A_ROPE)):
        parts[name] = wi[:, o:o + width]
        o += width
    used = COL_KG + A_ROPE + 2 * M_HEADS
    w_in_p = jnp.concatenate([parts[k] for k in ("mq", "mk", "mv", "mo", "cq", "ckv", "kpe", "mi", "mf")]
                             + [jnp.zeros((d, N_IN_PAD - used), F32)], axis=1).astype(BF16)
    wuq = w_uq[l].reshape(Q_RANK, A_HEADS, A_NOPE + A_ROPE)
    wuq = jnp.pad(wuq, ((0, 0), (0, 0), (0, QK_HEAD - A_NOPE - A_ROPE))).reshape(Q_RANK, A_HEADS * QK_HEAD)
    bias = jnp.zeros((LANE,), F32).at[GATE_I:GATE_I + 2 * M_HEADS].set(b_if[l].astype(F32)).reshape(1, LANE)
    wuk = w_uk[l].astype(BF16)
    return {
        "g_mix": g_mix_norm[l].reshape(1, d), "w_in": w_in_p, "bias": bias,
        "g_mh": g_mh[l].reshape(1, M_WIDTH), "g_cq": g_cq[l].reshape(1, Q_RANK),
        "w_uq": wuq.astype(BF16), "g_ckv": g_ckv[l].reshape(1, KV_RANK),
        "w_uk": wuk, "w_ukt": wuk.T, "w_uv": w_uv[l].astype(BF16),
        "g_qn": g_qn[l].reshape(1, A_NOPE), "g_qr": _pad_lanes(g_qr[l]),
        "g_kn": g_kn[l].reshape(1, A_NOPE), "g_kr": _pad_lanes(g_kr[l]),
        "w_out": w_out[l].astype(BF16), "g_ffn": g_ffn_norm[l].reshape(1, d),
        "g_ple": g_ple_norm[l].reshape(1, d), "w_ple_gate": w_ple_gate[l].astype(BF16),
        "w_ple_proj": w_ple_proj[l].astype(BF16),
    }


def _row_tile(t, pref):
    return pref if t % pref == 0 else t


def kernel(x_prompt, x_sample, p_prompt, p_sample, cache_ckv, cache_kpe, state_C, state_n, state_m, page_table, g_mix_norm, w_in, b_if, g_mh, g_cq, w_uq, g_ckv, w_uk, w_uv, g_qn, g_qr, g_kn, g_kr, w_out, g_ffn_norm, w_gate, w_up, w_down, w_router, w_gate_e, w_up_e, w_down_e, g_ple_norm, w_ple_gate, w_ple_proj):
    depth = w_in.shape[0]
    bp, sp, d = x_prompt.shape
    bs, ss, _ = x_sample.shape
    n_pages = page_table.shape[1]
    past = n_pages * PAGE_SIZE
    tp = bp * sp
    ts = bs * ss
    tm_p = _row_tile(sp, 512)
    tm_s = ts
    chunk = M_CHUNK if sp % M_CHUNK == 0 else sp
    pad_s = 16

    cos_p, sin_p = _rope_tables(jnp.arange(sp))
    cos_s, sin_s = _rope_tables(past + jnp.arange(ss))
    cos_s = jnp.broadcast_to(cos_s[None], (bs, ss, LANE)).reshape(ts, LANE)
    sin_s = jnp.broadcast_to(sin_s[None], (bs, ss, LANE)).reshape(ts, LANE)

    y_p = x_prompt.reshape(tp, d)
    y_s = x_sample.reshape(ts, d)
    outs = {k: [] for k in ("ckv_p", "kpe_p", "C_p", "n_p", "m_p", "ckv_s", "kpe_s", "C_s", "n_s", "m_s")}
    zc = jnp.zeros((bp, M_HEADS, M_DQK, M_DV), F32)
    zn = jnp.zeros((bp, M_HEADS, 1, M_DQK), F32)
    zm = jnp.zeros((bp, M_HEADS, 1, LANE), F32)

    for l in range(depth):
        lw = _layer_weights(l, g_mix_norm, w_in, b_if, g_mh, g_cq, w_uq, g_ckv, w_uk, w_uv, g_qn, g_qr, g_kn,
                            g_kr, w_out, g_ffn_norm, g_ple_norm, w_ple_gate, w_ple_proj)

        z_p = norm_matmul(y_p, lw["g_mix"], lw["w_in"], tm_p, 1024)
        ckv_p, kpe_p, q_p, k_p, v_p = mla_prep(z_p, cos_p, sin_p, lw, tm_p, sp // tm_p, False)
        hm_p, c1, n1, m1 = mlstm(z_p, lw["bias"], lw["g_mh"], zc, zn, zm, bp, sp // chunk, chunk, chunk)
        ha_p = prompt_attention(q_p, k_p, v_p, bp, sp, tm_p)
        y_p = out_proj(hm_p, ha_p, lw["w_out"], y_p, tm_p, 1024)
        outs["ckv_p"].append(ckv_p.reshape(bp, sp, KV_RANK))
        outs["kpe_p"].append(kpe_p.reshape(bp, sp, A_ROPE))
        outs["C_p"].append(c1)
        outs["n_p"].append(n1.reshape(bp, M_HEADS, M_DQK))
        outs["m_p"].append(m1[:, :, 0, 0])

        z_s = norm_matmul(y_s, lw["g_mix"], lw["w_in"], tm_s, 1024)
        ckv_s, kpe_s, q_s, k_s, _, a_s = mla_prep(z_s, cos_s, sin_s, lw, tm_s, 1, True)
        z_s8 = jnp.pad(z_s.reshape(ts, 1, N_IN_PAD), ((0, 0), (pad_s - 1, 0), (0, 0))).reshape(ts * pad_s, N_IN_PAD)
        hm_s8, c2, n2, m2 = mlstm(z_s8, lw["bias"], lw["g_mh"], state_C[l],
                                  state_n[l].reshape(bs, M_HEADS, 1, M_DQK),
                                  jnp.broadcast_to(state_m[l][:, :, None, None], (bs, M_HEADS, 1, LANE)),
                                  ts, 1, pad_s, 1)
        hm_s = hm_s8.reshape(ts, pad_s, M_WIDTH)[:, pad_s - 1, :]
        ha_s = decode_attention(l, page_table, cache_ckv, cache_kpe,
                                q_s.reshape(ts, A_HEADS, QK_HEAD), k_s.reshape(ts, A_HEADS, QK_HEAD),
                                a_s.reshape(ts, A_HEADS, KV_RANK), ckv_s.reshape(ts, 1, KV_RANK),
                                lw["w_ukt"], lw["w_uv"], 8 if n_pages % 8 == 0 else 1)
        y_s = out_proj(hm_s, ha_s.reshape(ts, A_WIDTH).astype(BF16), lw["w_out"], y_s, tm_s, 1024)
        outs["ckv_s"].append(ckv_s.reshape(bs, ss, KV_RANK))
        outs["kpe_s"].append(kpe_s.reshape(bs, ss, A_ROPE))
        outs["C_s"].append(c2)
        outs["n_s"].append(n2.reshape(bs, M_HEADS, M_DQK))
        outs["m_s"].append(m2[:, :, 0, 0])

        if l % 2 == 0:
            wg, wu, wd = (w[l // 2].astype(BF16) for w in (w_gate, w_up, w_down))
            tf = 512 if wg.shape[1] % 512 == 0 else wg.shape[1]
            y_p = ffn_dense(y_p, lw["g_ffn"], wg, wu, wd, tm_p, tf)
            y_s = ffn_dense(y_s, lw["g_ffn"], wg, wu, wd, tm_s, tf)
        else:
            wg, wu, wd = (w[l // 2].astype(BF16) for w in (w_gate_e, w_up_e, w_down_e))
            wr = jnp.pad(w_router[l // 2].astype(F32), ((0, 0), (0, LANE - N_EXPERTS)))
            tf = 512 if wg.shape[2] % 512 == 0 else wg.shape[2]
            y_p = moe_mixer(y_p, lw["g_ffn"], wr, wg, wu, wd, min(tm_p, 256), tm_p, tf)
            y_s = moe_mixer(y_s, lw["g_ffn"], wr, wg, wu, wd, tm_s, 128, tf)
        y_p = ple(y_p, lw["g_ple"], p_prompt[l].reshape(tp, -1), lw["w_ple_gate"], lw["w_ple_proj"], tm_p, 1024)
        y_s = ple(y_s, lw["g_ple"], p_sample[l].reshape(ts, -1), lw["w_ple_gate"], lw["w_ple_proj"], tm_s, 1024)

    st = lambda k: jnp.stack(outs[k])
    return (y_p.reshape(bp, sp, d), y_s.reshape(bs, ss, d),
            st("ckv_p"), st("kpe_p"), st("C_p"), st("n_p"), st("m_p"),
            st("ckv_s"), st("kpe_s"), st("C_s"), st("n_s"), st("m_s"))
```

```python
import functools

import jax
import jax.numpy as jnp
from jax import lax
from jax.experimental import pallas as pl
from jax.experimental.pallas import tpu as pltpu

F32 = jnp.float32
BF16 = jnp.bfloat16

M_HEADS = 4
M_DQK = 128
M_DV = 256
M_WIDTH = M_HEADS * M_DV
M_CHUNK = 64
MLSTM_CHUNK = 256
A_HEADS = 8
A_NOPE = 128
A_ROPE = 64
A_VD = 128
A_WIDTH = A_HEADS * A_VD
Q_RANK = 512
KV_RANK = 256
ROPE_THETA = 10000.0
MLA_SCALE = (A_NOPE + A_ROPE) ** -0.5
PAGE_SIZE = 128
N_EXPERTS = 8
EPS = 1e-6

LANE = 128
MIB = 2 ** 20

COL_MQ = 0
COL_MK = COL_MQ + M_HEADS * M_DQK
COL_MV = COL_MK + M_HEADS * M_DQK
COL_MO = COL_MV + M_WIDTH
COL_CQ = COL_MO + M_WIDTH
COL_CKV = COL_CQ + Q_RANK
COL_KG = COL_CKV + KV_RANK
N_IN_PAD = 4096
GATE_I = A_ROPE
GATE_F = A_ROPE + M_HEADS
QK_HEAD = 2 * LANE
NEG_BIG = -1e30


def _cparams(sem, vmem_mib):
    return pltpu.CompilerParams(dimension_semantics=sem, vmem_limit_bytes=vmem_mib * MIB)


def _rms(x, g, n):
    ms = jnp.sum(x * x, axis=-1, keepdims=True) * (1.0 / n)
    return x * lax.rsqrt(ms + EPS) * g


def _dot(a, b):
    return jnp.dot(a, b, preferred_element_type=F32)


def _dot_nt(a, b):
    return lax.dot_general(a, b, (((1,), (1,)), ((), ())), preferred_element_type=F32)


def _dot_tn(a, b):
    return lax.dot_general(a, b, (((0,), (0,)), ((), ())), preferred_element_type=F32)


def _norm_matmul_kernel(x_ref, g_ref, w_ref, o_ref, xn_ref):
    @pl.when(pl.program_id(1) == 0)
    def _():
        xn_ref[...] = _rms(x_ref[...], g_ref[...], x_ref.shape[-1]).astype(BF16)

    o_ref[...] = _dot(xn_ref[...], w_ref[...])


def norm_matmul(x, g, w, tm, tn):
    t, d = x.shape
    n = w.shape[1]
    return pl.pallas_call(
        _norm_matmul_kernel,
        grid=(t // tm, n // tn),
        in_specs=[pl.BlockSpec((tm, d), lambda i, j: (i, 0)),
                  pl.BlockSpec((1, d), lambda i, j: (0, 0)),
                  pl.BlockSpec((d, tn), lambda i, j: (0, j))],
        out_specs=pl.BlockSpec((tm, tn), lambda i, j: (i, j)),
        out_shape=jax.ShapeDtypeStruct((t, n), F32),
        scratch_shapes=[pltpu.VMEM((tm, d), BF16)],
        compiler_params=_cparams(("parallel", "arbitrary"), 40),
        name="norm_matmul",
    )(x, g, w)


def _mla_prep_kernel(cq_ref, ckv_ref, kg_ref, cos_ref, sin_ref, wuq_ref, wuk_ref, wuv_ref,
                     gcq_ref, gckv_ref, gkr_ref, gqn_ref, gqr_ref, gkn_ref,
                     ckv_out, kpe_out, q_out, k_out, v_out, *a_out, absorb):
    cos = cos_ref[...]
    sin = sin_ref[...]

    def rope(x):
        return x * cos + (pltpu.roll(x, A_ROPE // 2, 1) + pltpu.roll(x, LANE - A_ROPE // 2, 1)) * sin

    ckv = _rms(ckv_ref[...], gckv_ref[...], KV_RANK)
    ckv_out[...] = ckv
    ckv_b = ckv.astype(BF16)

    kg = kg_ref[...]
    lane = lax.broadcasted_iota(jnp.int32, kg.shape, 1)
    kp = jnp.where(lane < A_ROPE, kg, 0.0)
    kpr = rope(_rms(kp, gkr_ref[...], A_ROPE))
    kpe_out[...] = kpr[:, :A_ROPE]
    kpr_b = kpr.astype(BF16)

    kn = _dot(ckv_b, wuk_ref[...])
    v_out[...] = _dot(ckv_b, wuv_ref[...]).astype(BF16)
    cq = _rms(cq_ref[...], gcq_ref[...], Q_RANK).astype(BF16)
    qf = _dot(cq, wuq_ref[...])
    gkn = gkn_ref[...]
    for h in range(A_HEADS):
        n0 = h * A_NOPE
        c0 = h * QK_HEAD
        k_out[:, c0:c0 + LANE] = _rms(kn[:, n0:n0 + A_NOPE], gkn, A_NOPE).astype(BF16)
        k_out[:, c0 + LANE:c0 + QK_HEAD] = kpr_b
        qn = _rms(qf[:, c0:c0 + LANE], gqn_ref[...], A_NOPE)
        q_out[:, c0:c0 + LANE] = qn.astype(BF16)
        qp = _rms(qf[:, c0 + LANE:c0 + QK_HEAD], gqr_ref[...], A_ROPE)
        q_out[:, c0 + LANE:c0 + QK_HEAD] = rope(qp).astype(BF16)
        if absorb:
            a_out[0][:, h * KV_RANK:(h + 1) * KV_RANK] = _dot_nt(
                (qn * gkn).astype(BF16), wuk_ref[:, n0:n0 + A_NOPE])


def mla_prep(z, cos, sin, lw, tm, n_pos_blocks, absorb):
    t = z.shape[0]
    row = lambda i: (i, 0)
    full = lambda i: (0, 0)
    out_shape = [jax.ShapeDtypeStruct((t, KV_RANK), F32),
                 jax.ShapeDtypeStruct((t, A_ROPE), F32),
                 jax.ShapeDtypeStruct((t, A_HEADS * QK_HEAD), BF16),
                 jax.ShapeDtypeStruct((t, A_HEADS * QK_HEAD), BF16),
                 jax.ShapeDtypeStruct((t, A_WIDTH), BF16)]
    out_specs = [pl.BlockSpec((tm, KV_RANK), row), pl.BlockSpec((tm, A_ROPE), row),
                 pl.BlockSpec((tm, A_HEADS * QK_HEAD), row), pl.BlockSpec((tm, A_HEADS * QK_HEAD), row),
                 pl.BlockSpec((tm, A_WIDTH), row)]
    if absorb:
        out_shape.append(jax.ShapeDtypeStruct((t, A_HEADS * KV_RANK), F32))
        out_specs.append(pl.BlockSpec((tm, A_HEADS * KV_RANK), row))
    return pl.pallas_call(
        functools.partial(_mla_prep_kernel, absorb=absorb),
        grid=(t // tm,),
        in_specs=[pl.BlockSpec((tm, Q_RANK), lambda i: (i, COL_CQ // Q_RANK)),
                  pl.BlockSpec((tm, KV_RANK), lambda i: (i, COL_CKV // KV_RANK)),
                  pl.BlockSpec((tm, LANE), lambda i: (i, COL_KG // LANE)),
                  pl.BlockSpec((tm, LANE), lambda i: (i % n_pos_blocks, 0)),
                  pl.BlockSpec((tm, LANE), lambda i: (i % n_pos_blocks, 0)),
                  pl.BlockSpec((Q_RANK, A_HEADS * QK_HEAD), full),
                  pl.BlockSpec((KV_RANK, A_HEADS * A_NOPE), full),
                  pl.BlockSpec((KV_RANK, A_WIDTH), full),
                  pl.BlockSpec((1, Q_RANK), full), pl.BlockSpec((1, KV_RANK), full),
                  pl.BlockSpec((1, LANE), full), pl.BlockSpec((1, LANE), full),
                  pl.BlockSpec((1, LANE), full), pl.BlockSpec((1, LANE), full)],
        out_specs=out_specs,
        out_shape=out_shape,
        compiler_params=_cparams(("parallel",), 48),
        name="mla_prep",
    )(z, z, z, cos, sin, lw["w_uq"], lw["w_uk"], lw["w_uv"], lw["g_cq"], lw["g_ckv"],
      lw["g_kr"], lw["g_qn"], lw["g_qr"], lw["g_kn"])


def _mlstm_kernel(q_ref, k_ref, v_ref, o_ref, kg_ref, bias_ref, gmh_ref, c0_ref, n0_ref, m0_ref,
                  h_out, c_out, n_out, m_out, c_s, n_s, m_s, *, chunk, n_real):
    c = pl.program_id(1)

    @pl.when(c == 0)
    def _():
        c_s[...] = c0_ref[...]
        n_s[...] = n0_ref[...]
        m_s[...] = m0_ref[...]

    L = chunk
    gates = kg_ref[...] + bias_ref[...]
    lf_all = jax.nn.log_sigmoid(gates)
    ri = lax.broadcasted_iota(jnp.int32, (L, L), 0)
    ci = lax.broadcasted_iota(jnp.int32, (L, L), 1)
    tri = ci <= ri
    eye = ci == ri
    rowid = lax.broadcasted_iota(jnp.int32, (L, 1), 0)
    real = rowid >= (L - n_real)
    scale = M_DQK ** -0.5

    def to_row(col):
        return jnp.sum(jnp.where(eye, col, 0.0), axis=0, keepdims=True)

    for h in range(M_HEADS):
        i_col = gates[:, GATE_I + h:GATE_I + h + 1]
        lf_col = lf_all[:, GATE_F + h:GATE_F + h + 1]
        if n_real < L:
            i_col = jnp.where(real, i_col, NEG_BIG)
            lf_col = jnp.where(real, lf_col, 0.0)
        i_row = to_row(i_col)
        lf_row = to_row(lf_col)
        bh_col = jnp.sum(jnp.where(tri, lf_row, 0.0), axis=1, keepdims=True)
        bh_row = to_row(bh_col)
        dmat = jnp.where(tri, bh_col - bh_row + i_row, -jnp.inf)
        m_prev = m_s[h][:, 0:1]
        inter = bh_col + m_prev
        m_t = jnp.maximum(inter, jnp.max(dmat, axis=1, keepdims=True))
        w_inter = jnp.exp(inter - m_t)
        qh = q_ref[:, h * M_DQK:(h + 1) * M_DQK]
        kh = k_ref[:, h * M_DQK:(h + 1) * M_DQK]
        vh = v_ref[:, h * M_DV:(h + 1) * M_DV]
        qb = qh.astype(BF16)
        vb = vh.astype(BF16)
        s_qk = _dot_nt(qb, kh.astype(BF16)) * scale * jnp.exp(dmat - m_t)
        c_prev = c_s[h]
        n_prev = n_s[h]
        num = _dot(s_qk.astype(BF16), vb) + _dot(qb, c_prev.astype(BF16)) * (scale * w_inter)
        qn = jnp.sum(qh * n_prev, axis=1, keepdims=True) * scale
        den = jnp.sum(s_qk, axis=1, keepdims=True) + w_inter * qn
        den = jnp.maximum(jnp.abs(den), jnp.exp(-m_t))
        hh = num / den
        m_last = m_t[L - 1:L, :]
        bh_last = bh_col[L - 1:L, :]
        g_inter = jnp.exp(bh_last + m_prev - m_last)
        g_intra = jnp.exp(bh_last - bh_col + i_col - m_last)
        kw = kh * g_intra
        c_s[h] = g_inter * c_prev + _dot_tn(kw.astype(BF16), vb)
        n_s[h] = g_inter * n_prev + jnp.sum(kw, axis=0, keepdims=True)
        m_s[h] = jnp.broadcast_to(m_last, (1, LANE))
        hn = _rms(hh, gmh_ref[:, h * M_DV:(h + 1) * M_DV], M_DV)
        og = o_ref[:, h * M_DV:(h + 1) * M_DV]
        h_out[:, h * M_DV:(h + 1) * M_DV] = (hn * jax.nn.sigmoid(og)).astype(BF16)

    @pl.when(c == pl.num_programs(1) - 1)
    def _():
        c_out[...] = c_s[...]
        n_out[...] = n_s[...]
        m_out[...] = m_s[...]


def mlstm(z, bias, gmh, c0, n0, m0, batch, n_chunks, chunk, n_real):
    t = z.shape[0]
    blk = lambda w, col: pl.BlockSpec((chunk, w), lambda b, c: (b * n_chunks + c, col))
    st = lambda a: pl.BlockSpec((None,) + a.shape[1:], lambda b, c: (b, 0, 0, 0))
    full = lambda b, c: (0, 0)
    return pl.pallas_call(
        functools.partial(_mlstm_kernel, chunk=chunk, n_real=n_real),
        grid=(batch, n_chunks),
        in_specs=[blk(M_HEADS * M_DQK, COL_MQ // (M_HEADS * M_DQK)),
                  blk(M_HEADS * M_DQK, COL_MK // (M_HEADS * M_DQK)),
                  blk(M_WIDTH, COL_MV // M_WIDTH), blk(M_WIDTH, COL_MO // M_WIDTH),
                  blk(LANE, COL_KG // LANE),
                  pl.BlockSpec((1, LANE), full), pl.BlockSpec((1, M_WIDTH), full),
                  st(c0), st(n0), st(m0)],
        out_specs=[pl.BlockSpec((chunk, M_WIDTH), lambda b, c: (b * n_chunks + c, 0)),
                   st(c0), st(n0), st(m0)],
        out_shape=[jax.ShapeDtypeStruct((t, M_WIDTH), BF16),
                   jax.ShapeDtypeStruct(c0.shape, F32), jax.ShapeDtypeStruct(n0.shape, F32),
                   jax.ShapeDtypeStruct(m0.shape, F32)],
        scratch_shapes=[pltpu.VMEM(c0.shape[1:], F32), pltpu.VMEM(n0.shape[1:], F32),
                        pltpu.VMEM(m0.shape[1:], F32)],
        compiler_params=_cparams(("parallel", "arbitrary"), 32),
        name="mlstm",
    )(z, z, z, z, z, bias, gmh, c0, n0, m0)


def _attn_kernel(q_ref, k_ref, v_ref, o_ref, *, tq):
    qi = pl.program_id(2)
    q = q_ref[...]
    row = lax.broadcasted_iota(jnp.int32, (tq, tq), 0) + qi * tq
    col0 = lax.broadcasted_iota(jnp.int32, (tq, tq), 1)

    def body(j, carry):
        m, l, acc = carry
        start = pl.multiple_of(j * tq, tq)
        kb = k_ref[pl.ds(start, tq), :]
        vb = v_ref[pl.ds(start, tq), :]
        s = _dot_nt(q, kb) * MLA_SCALE
        s = jnp.where(col0 + j * tq <= row, s, -jnp.inf)
        m_new = jnp.maximum(m, jnp.max(s, axis=1, keepdims=True))
        alpha = jnp.exp(m - m_new)
        p = jnp.exp(s - m_new)
        l = alpha * l + jnp.sum(p, axis=1, keepdims=True)
        acc = alpha * acc + _dot(p.astype(BF16), vb)
        return m_new, l, acc

    init = (jnp.full((tq, 1), -jnp.inf, F32), jnp.zeros((tq, 1), F32), jnp.zeros((tq, A_VD), F32))
    _, l, acc = lax.fori_loop(0, qi + 1, body, init)
    o_ref[...] = (acc / l).astype(o_ref.dtype)


def prompt_attention(q, k, v, batch, seq, tq):
    nq = seq // tq
    return pl.pallas_call(
        functools.partial(_attn_kernel, tq=tq),
        grid=(batch, A_HEADS, nq),
        in_specs=[pl.BlockSpec((tq, QK_HEAD), lambda b, h, i: (b * nq + i, h)),
                  pl.BlockSpec((seq, QK_HEAD), lambda b, h, i: (b, h)),
                  pl.BlockSpec((seq, A_VD), lambda b, h, i: (b, h))],
        out_specs=pl.BlockSpec((tq, A_VD), lambda b, h, i: (b * nq + i, h)),
        out_shape=jax.ShapeDtypeStruct((batch * seq, A_WIDTH), BF16),
        compiler_params=_cparams(("parallel", "parallel", "arbitrary"), 32),
        name="prompt_attention",
    )(q, k, v)


DECODE_SLOTS = 3


def _decode_attn_kernel(pt_ref, q_ref, knew_ref, a_ref, cnew_ref, wukt_ref, wuv_ref, ckv_hbm, kpe_hbm, o_ref,
                        lhs_s, cbuf, kbuf, sem_c, sem_k, *, layer, pps, n_chunks):
    b = pl.program_id(0)
    nb = pl.num_programs(0)
    total = nb * n_chunks
    n_up = A_HEADS * A_NOPE
    tc = pps * PAGE_SIZE

    def start_chunk(g):
        slot = g % DECODE_SLOTS
        gg = jnp.minimum(g, total - 1)
        gb = gg // n_chunks
        gc = gg % n_chunks
        for p in range(pps):
            page = pt_ref[gb, gc * pps + p]
            pltpu.make_async_copy(ckv_hbm.at[layer, page], cbuf.at[slot, p], sem_c.at[slot]).start()
            pltpu.make_async_copy(kpe_hbm.at[layer, page], kbuf.at[slot, p], sem_k.at[slot]).start()

    def wait_chunk(slot):
        pltpu.make_async_copy(ckv_hbm.at[layer, pl.ds(0, pps)], cbuf.at[slot], sem_c.at[slot]).wait()
        pltpu.make_async_copy(kpe_hbm.at[layer, pl.ds(0, pps)], kbuf.at[slot], sem_k.at[slot]).wait()

    @pl.when(b == 0)
    def _():
        lhs_s[0:n_up, :] = wukt_ref[...]
        start_chunk(0)
        start_chunk(1)

    lhs_s[n_up:, :] = jnp.concatenate(
        [a_ref[...], jnp.zeros((lhs_s.shape[0] - n_up - A_HEADS, KV_RANK), F32)], axis=0).astype(BF16)
    q = q_ref[...]
    qpe = q[:, LANE:LANE + A_ROPE]
    s_new = jnp.sum(q.astype(F32) * knew_ref[...].astype(F32), axis=1, keepdims=True) * MLA_SCALE
    hid = lax.broadcasted_iota(jnp.int32, (A_HEADS, tc), 0)

    def body(c, carry):
        m_old, l_old, acc_old = carry
        g = b * n_chunks + c
        start_chunk(g + 2)
        slot = g % DECODE_SLOTS
        wait_chunk(slot)
        cb = cbuf[slot].reshape(tc, KV_RANK).astype(BF16)
        up = _dot_nt(lhs_s[...], cb)
        ss = jnp.zeros((A_HEADS, tc), F32)
        for h in range(A_HEADS):
            blk = up[h * A_NOPE:(h + 1) * A_NOPE, :]
            ss = jnp.where(hid == h, jnp.sum(blk * blk, axis=0, keepdims=True), ss)
        s_nope = up[n_up:n_up + A_HEADS, :] * lax.rsqrt(ss * (1.0 / A_NOPE) + EPS)
        s_pe = jnp.concatenate([_dot(qpe, kbuf[slot, p].astype(BF16)) for p in range(pps)], axis=1)
        s = (s_nope + s_pe) * MLA_SCALE
        m_new = jnp.maximum(m_old, jnp.max(s, axis=1, keepdims=True))
        alpha = jnp.exp(m_old - m_new)
        p_att = jnp.exp(s - m_new)
        l_new = alpha * l_old + jnp.sum(p_att, axis=1, keepdims=True)
        acc_new = alpha * acc_old + _dot(p_att.astype(BF16), cb)
        return m_new, l_new, acc_new

    init = (s_new, jnp.ones((A_HEADS, 1), F32), jnp.broadcast_to(cnew_ref[...], (A_HEADS, KV_RANK)))
    _, l_fin, acc_fin = lax.fori_loop(0, n_chunks, body, init, unroll=2)

    o_lat = acc_fin / l_fin
    full = _dot(o_lat.astype(BF16), wuv_ref[...])
    hid_o = lax.broadcasted_iota(jnp.int32, (A_HEADS, A_VD), 0)
    res = jnp.zeros((A_HEADS, A_VD), F32)
    for h in range(A_HEADS):
        res = jnp.where(hid_o == h, full[:, h * A_VD:(h + 1) * A_VD], res)
    o_ref[...] = res

    @pl.when(b == nb - 1)
    def _():
        wait_chunk(total % DECODE_SLOTS)
        wait_chunk((total + 1) % DECODE_SLOTS)


def decode_attention(layer, page_table, cache_ckv, cache_kpe_t, q3, k3, a3, cnew3, wukt, wuv, pps):
    nb, n_pages = page_table.shape
    n_up = A_HEADS * A_NOPE
    per_seq = lambda rows, w: pl.BlockSpec((None, rows, w), lambda b, pt: (b, 0, 0))
    full = lambda b, pt: (0, 0)
    return pl.pallas_call(
        functools.partial(_decode_attn_kernel, layer=layer, pps=pps, n_chunks=n_pages // pps),
        grid_spec=pltpu.PrefetchScalarGridSpec(
            num_scalar_prefetch=1,
            grid=(nb,),
            in_specs=[per_seq(A_HEADS, QK_HEAD), per_seq(A_HEADS, QK_HEAD), per_seq(A_HEADS, KV_RANK),
                      per_seq(1, KV_RANK),
                      pl.BlockSpec((n_up, KV_RANK), full), pl.BlockSpec((KV_RANK, A_WIDTH), full),
                      pl.BlockSpec(memory_space=pl.ANY), pl.BlockSpec(memory_space=pl.ANY)],
            out_specs=pl.BlockSpec((None, A_HEADS, A_VD), lambda b, pt: (b, 0, 0)),
            scratch_shapes=[pltpu.VMEM((n_up + 16, KV_RANK), BF16),
                            pltpu.VMEM((DECODE_SLOTS, pps, PAGE_SIZE, KV_RANK), F32),
                            pltpu.VMEM((DECODE_SLOTS, pps, A_ROPE, PAGE_SIZE), F32),
                            pltpu.SemaphoreType.DMA((DECODE_SLOTS,)),
                            pltpu.SemaphoreType.DMA((DECODE_SLOTS,))]),
        out_shape=jax.ShapeDtypeStruct((nb, A_HEADS, A_VD), F32),
        compiler_params=_cparams(("arbitrary",), 48),
        name="decode_attention",
    )(page_table, q3, k3, a3, cnew3, wukt, wuv, cache_ckv, cache_kpe_t)


def _out_proj_kernel(hm_ref, ha_ref, w_ref, x_ref, o_ref):
    acc = _dot(hm_ref[...], w_ref[0:M_WIDTH, :]) + _dot(ha_ref[...], w_ref[M_WIDTH:, :])
    o_ref[...] = x_ref[...] + acc


def out_proj(hm, ha, w, x, tm, tn):
    t, d = x.shape
    return pl.pallas_call(
        _out_proj_kernel,
        grid=(t // tm, d // tn),
        in_specs=[pl.BlockSpec((tm, M_WIDTH), lambda i, j: (i, 0)),
                  pl.BlockSpec((tm, A_WIDTH), lambda i, j: (i, 0)),
                  pl.BlockSpec((M_WIDTH + A_WIDTH, tn), lambda i, j: (0, j)),
                  pl.BlockSpec((tm, tn), lambda i, j: (i, j))],
        out_specs=pl.BlockSpec((tm, tn), lambda i, j: (i, j)),
        out_shape=jax.ShapeDtypeStruct((t, d), F32),
        compiler_params=_cparams(("parallel", "arbitrary"), 40),
        name="out_proj",
    )(hm, ha, w, x)


def _ffn_kernel(x_ref, g_ref, wg_ref, wu_ref, wd_ref, o_ref, xn_ref, acc_ref):
    f = pl.program_id(1)

    @pl.when(f == 0)
    def _():
        xn_ref[...] = _rms(x_ref[...], g_ref[...], x_ref.shape[-1]).astype(BF16)
        acc_ref[...] = jnp.zeros(acc_ref.shape, F32)

    xn = xn_ref[...]
    a = jax.nn.silu(_dot(xn, wg_ref[...])) * _dot(xn, wu_ref[...])
    acc_ref[...] += _dot(a.astype(BF16), wd_ref[...])

    @pl.when(f == pl.num_programs(1) - 1)
    def _():
        o_ref[...] = x_ref[...] + acc_ref[...]


def ffn_dense(x, g, wg, wu, wd, tm, tf):
    t, d = x.shape
    ff = wg.shape[1]
    return pl.pallas_call(
        _ffn_kernel,
        grid=(t // tm, ff // tf),
        in_specs=[pl.BlockSpec((tm, d), lambda i, f: (i, 0)),
                  pl.BlockSpec((1, d), lambda i, f: (0, 0)),
                  pl.BlockSpec((d, tf), lambda i, f: (0, f)),
                  pl.BlockSpec((d, tf), lambda i, f: (0, f)),
                  pl.BlockSpec((tf, d), lambda i, f: (f, 0))],
        out_specs=pl.BlockSpec((tm, d), lambda i, f: (i, 0)),
        out_shape=jax.ShapeDtypeStruct((t, d), F32),
        scratch_shapes=[pltpu.VMEM((tm, d), BF16), pltpu.VMEM((tm, d), F32)],
        compiler_params=_cparams(("parallel", "arbitrary"), 48),
        name="ffn_dense",
    )(x, g, wg, wu, wd)


def _ple_kernel(x_ref, g_ref, p_ref, wg_ref, wp_ref, o_ref, xn_ref, *, tn):
    j = pl.program_id(1)

    @pl.when(j == 0)
    def _():
        xn_ref[...] = _rms(x_ref[...], g_ref[...], x_ref.shape[-1]).astype(BF16)

    gate = jax.nn.sigmoid(_dot(xn_ref[...], wg_ref[...]))
    proj = _dot(p_ref[...].astype(BF16), wp_ref[...])
    o_ref[...] = x_ref[:, pl.ds(pl.multiple_of(j * tn, tn), tn)] + gate * proj


def ple(x, g, p, wg, wp, tm, tn):
    t, d = x.shape
    pd = p.shape[1]
    return pl.pallas_call(
        functools.partial(_ple_kernel, tn=tn),
        grid=(t // tm, d // tn),
        in_specs=[pl.BlockSpec((tm, d), lambda i, j: (i, 0)),
                  pl.BlockSpec((1, d), lambda i, j: (0, 0)),
                  pl.BlockSpec((tm, pd), lambda i, j: (i, 0)),
                  pl.BlockSpec((d, tn), lambda i, j: (0, j)),
                  pl.BlockSpec((pd, tn), lambda i, j: (0, j))],
        out_specs=pl.BlockSpec((tm, tn), lambda i, j: (i, j)),
        out_shape=jax.ShapeDtypeStruct((t, d), F32),
        scratch_shapes=[pltpu.VMEM((tm, d), BF16)],
        compiler_params=_cparams(("parallel", "arbitrary"), 40),
        name="ple",
    )(x, g, p, wg, wp)


def _router_kernel(x_ref, g_ref, wr_ref, idx_out, gate_out, cnt_out, carry_s):
    i = pl.program_id(0)

    @pl.when(i == 0)
    def _():
        carry_s[...] = jnp.zeros(carry_s.shape, F32)

    hn = _rms(x_ref[...], g_ref[...], x_ref.shape[-1])
    logits = _dot(hn.astype(BF16), wr_ref[...])
    tm = logits.shape[0]
    lane = lax.broadcasted_iota(jnp.int32, logits.shape, 1)
    lg = jnp.where(lane < N_EXPERTS, logits, -jnp.inf)
    m1 = jnp.max(lg, axis=1, keepdims=True)
    i1 = jnp.min(jnp.where(lg == m1, lane, LANE), axis=1, keepdims=True)
    lg2 = jnp.where(lane == i1, -jnp.inf, lg)
    m2 = jnp.max(lg2, axis=1, keepdims=True)
    i2 = jnp.min(jnp.where(lg2 == m2, lane, LANE), axis=1, keepdims=True)
    e2 = jnp.exp(m2 - m1)
    g1 = 1.0 / (1.0 + e2)
    g2 = e2 / (1.0 + e2)
    hit1 = lane == i1
    hit2 = lane == i2
    onehot = jnp.where(hit1 | hit2, 1.0, 0.0)
    ri = lax.broadcasted_iota(jnp.int32, (tm, tm), 0)
    ci = lax.broadcasted_iota(jnp.int32, (tm, tm), 1)
    before = jnp.where(ci < ri, 1.0, 0.0).astype(BF16)
    seen = carry_s[...] + _dot(before, onehot.astype(BF16))
    r1 = jnp.sum(jnp.where(hit1, seen, 0.0), axis=1, keepdims=True)
    r2 = jnp.sum(jnp.where(hit2, seen, 0.0), axis=1, keepdims=True)
    carry_s[...] += jnp.sum(onehot, axis=0, keepdims=True)
    idx = jnp.where(lane == 0, i1, jnp.where(lane == 1, i2, 0))
    rank = jnp.where(lane == 2, r1, jnp.where(lane == 3, r2, 0.0)).astype(jnp.int32)
    idx_out[...] = idx + rank
    gate_out[...] = jnp.where(lane == 0, g1, jnp.where(lane == 1, g2, 0.0))
    cnt_out[...] = carry_s[...]


def moe_router(x, g, wr, tm):
    t, d = x.shape
    row = lambda i: (i, 0)
    full = lambda i: (0, 0)
    return pl.pallas_call(
        _router_kernel,
        grid=(t // tm,),
        in_specs=[pl.BlockSpec((tm, d), row), pl.BlockSpec((1, d), full), pl.BlockSpec((d, LANE), full)],
        out_specs=[pl.BlockSpec((tm, LANE), row), pl.BlockSpec((tm, LANE), row),
                   pl.BlockSpec((1, LANE), full)],
        out_shape=[jax.ShapeDtypeStruct((t, LANE), jnp.int32),
                   jax.ShapeDtypeStruct((t, LANE), F32), jax.ShapeDtypeStruct((1, LANE), F32)],
        scratch_shapes=[pltpu.VMEM((1, LANE), F32)],
        compiler_params=_cparams(("arbitrary",), 40),
        name="moe_router",
    )(x, g, wr)


def _dispatch_kernel(dest_ref, x_ref, init_ref, xs_ref, sem, *, tm):
    del init_ref
    i = pl.program_id(0)

    def row_copy(r, k):
        slot = dest_ref[2 * (i * tm + r) + k]
        return pltpu.make_async_copy(x_ref.at[pl.ds(r, 1), :], xs_ref.at[pl.ds(slot, 1), :], sem)

    def start(r, carry):
        row_copy(r, 0).start()
        row_copy(r, 1).start()
        return carry

    def wait(r, carry):
        row_copy(r, 0).wait()
        row_copy(r, 1).wait()
        return carry

    lax.fori_loop(0, tm, start, 0)
    lax.fori_loop(0, tm, wait, 0)


def moe_dispatch(dest_flat, x, n_slots, tm):
    t, d = x.shape
    init = jnp.zeros((n_slots, d), F32)
    return pl.pallas_call(
        functools.partial(_dispatch_kernel, tm=tm),
        grid_spec=pltpu.PrefetchScalarGridSpec(
            num_scalar_prefetch=1,
            grid=(t // tm,),
            in_specs=[pl.BlockSpec((tm, d), lambda i, dr: (i, 0)), pl.BlockSpec(memory_space=pl.ANY)],
            out_specs=pl.BlockSpec(memory_space=pl.ANY),
            scratch_shapes=[pltpu.SemaphoreType.DMA(())]),
        out_shape=jax.ShapeDtypeStruct((n_slots, d), F32),
        input_output_aliases={2: 0},
        compiler_params=_cparams(("arbitrary",), 32),
        name="moe_dispatch",
    )(dest_flat, x, init)


def _expert_ffn_kernel(be_ref, nu_ref, x_ref, g_ref, wg_ref, wu_ref, wd_ref, o_ref, xn_ref, acc_ref):
    del be_ref
    i = pl.program_id(0)
    f = pl.program_id(1)
    used = i < nu_ref[0]

    @pl.when(f == 0)
    def _():
        xn_ref[...] = _rms(x_ref[...], g_ref[...], x_ref.shape[-1]).astype(BF16)
        acc_ref[...] = jnp.zeros(acc_ref.shape, F32)

    @pl.when(used)
    def _():
        x = xn_ref[...]
        a = jax.nn.silu(_dot(x, wg_ref[...])) * _dot(x, wu_ref[...])
        acc_ref[...] += _dot(a.astype(BF16), wd_ref[...])

    @pl.when(f == pl.num_programs(1) - 1)
    def _():
        o_ref[...] = acc_ref[...]


def expert_ffn(blk_e, n_used, xs, g, wg, wu, wd, blk, tf):
    p, d = xs.shape
    ff = wg.shape[2]
    nf = ff // tf

    def wcol(i, f, be, nu):
        live = i < nu[0]
        return (be[i], 0, jnp.where(live, f, nf - 1))

    def wrow(i, f, be, nu):
        live = i < nu[0]
        return (be[i], jnp.where(live, f, nf - 1), 0)

    return pl.pallas_call(
        _expert_ffn_kernel,
        grid_spec=pltpu.PrefetchScalarGridSpec(
            num_scalar_prefetch=2,
            grid=(p // blk, nf),
            in_specs=[pl.BlockSpec((blk, d), lambda i, f, be, nu: (i, 0)),
                      pl.BlockSpec((1, d), lambda i, f, be, nu: (0, 0)),
                      pl.BlockSpec((None, d, tf), wcol), pl.BlockSpec((None, d, tf), wcol),
                      pl.BlockSpec((None, tf, d), wrow)],
            out_specs=pl.BlockSpec((blk, d), lambda i, f, be, nu: (i, 0)),
            scratch_shapes=[pltpu.VMEM((blk, d), BF16), pltpu.VMEM((blk, d), F32)]),
        out_shape=jax.ShapeDtypeStruct((p, d), F32),
        compiler_params=_cparams(("arbitrary", "arbitrary"), 48),
        name="expert_ffn",
    )(blk_e, n_used, xs, g, wg, wu, wd)


def _combine_kernel(dest_ref, x_ref, gate_ref, ys_ref, o_ref, buf, sem):
    i = pl.program_id(0)
    n = pl.num_programs(0)
    tm = x_ref.shape[0]

    def row_copy(step, slot, r, k):
        src = dest_ref[2 * (step * tm + r) + k]
        return pltpu.make_async_copy(ys_ref.at[pl.ds(src, 1), :], buf.at[slot, k, pl.ds(r, 1), :], sem.at[slot])

    def start_block(step, slot):
        def body(r, carry):
            row_copy(step, slot, r, 0).start()
            row_copy(step, slot, r, 1).start()
            return carry
        lax.fori_loop(0, tm, body, 0)

    @pl.when(i == 0)
    def _():
        start_block(0, 0)

    @pl.when(i + 1 < n)
    def _():
        start_block(i + 1, (i + 1) % 2)

    slot = i % 2

    def wait_body(r, carry):
        row_copy(i, slot, r, 0).wait()
        row_copy(i, slot, r, 1).wait()
        return carry

    lax.fori_loop(0, tm, wait_body, 0)
    g = gate_ref[...]
    o_ref[...] = x_ref[...] + (buf[slot, 0] * g[:, 0:1] + buf[slot, 1] * g[:, 1:2])


def moe_combine(dest_flat, x, gates, ys, tm):
    t, d = x.shape
    return pl.pallas_call(
        _combine_kernel,
        grid_spec=pltpu.PrefetchScalarGridSpec(
            num_scalar_prefetch=1,
            grid=(t // tm,),
            in_specs=[pl.BlockSpec((tm, d), lambda i, dr: (i, 0)),
                      pl.BlockSpec((tm, LANE), lambda i, dr: (i, 0)),
                      pl.BlockSpec(memory_space=pl.ANY)],
            out_specs=pl.BlockSpec((tm, d), lambda i, dr: (i, 0)),
            scratch_shapes=[pltpu.VMEM((2, 2, tm, d), F32), pltpu.SemaphoreType.DMA((2,))]),
        out_shape=jax.ShapeDtypeStruct((t, d), F32),
        compiler_params=_cparams(("arbitrary",), 48),
        name="moe_combine",
    )(dest_flat, x, gates, ys)


def moe_mixer(x, g, wr, wg, wu, wd, tm, blk, tf):
    t, d = x.shape
    idx, gates, counts = moe_router(x, g, wr, tm)
    experts = idx[:, 0:2]
    ranks = idx[:, 2:4]
    cnt = counts[0, :N_EXPERTS].astype(jnp.int32)
    padded = (cnt + blk - 1) // blk * blk
    pend = jnp.cumsum(padded)
    pstart = pend - padded
    dest = (pstart[experts] + ranks).reshape(-1).astype(jnp.int32)
    n_blocks = -(-(2 * t) // blk) + N_EXPERTS
    n_used = (pend[-1] // blk).astype(jnp.int32)
    blk_ids = jnp.minimum(jnp.arange(n_blocks, dtype=jnp.int32), n_used - 1)
    blk_e = jnp.minimum(jnp.searchsorted(pend, blk_ids * blk, side="right"), N_EXPERTS - 1).astype(jnp.int32)
    xs = moe_dispatch(dest, x, n_blocks * blk, tm)
    ys = expert_ffn(blk_e, n_used.reshape(1), xs, g, wg, wu, wd, blk, tf)
    return moe_combine(dest, x, gates, ys, tm)


def _rope_tables(pos):
    half = A_ROPE // 2
    inv = ROPE_THETA ** (-jnp.arange(half, dtype=F32) / half)
    ang = pos.astype(F32)[:, None] * inv[None, :]
    cos, sin = jnp.cos(ang), jnp.sin(ang)
    zeros = jnp.zeros((pos.shape[0], LANE - A_ROPE), F32)
    return (jnp.concatenate([cos, cos, zeros], axis=-1), jnp.concatenate([-sin, sin, zeros], axis=-1))


def _pad_lanes(v, width=LANE):
    return jnp.pad(v, (0, width - v.shape[0])).reshape(1, width)


def _layer_weights(l, g_mix_norm, w_in, b_if, g_mh, g_cq, w_uq, g_ckv, w_uk, w_uv, g_qn, g_qr, g_kn, g_kr,
                   w_out, g_ffn_norm, g_ple_norm, w_ple_gate, w_ple_proj):
    d = w_in.shape[1]
    wi = w_in[l]
    o = 0
    parts = {}
    for name, width in (("mq", M_HEADS * M_DQK), ("mk", M_HEADS * M_DQK), ("mv", M_WIDTH), ("mo", M_WIDTH),
                        ("mi", M_HEADS), ("mf", M_HEADS), ("cq", Q_RANK), ("ckv", KV_RANK), ("kpe", A_ROPE)):
        parts[name] = wi[:, o:o + width]
        o += width
    used = COL_KG + A_ROPE + 2 * M_HEADS
    w_in_p = jnp.concatenate([parts[k] for k in ("mq", "mk", "mv", "mo", "cq", "ckv", "kpe", "mi", "mf")]
                             + [jnp.zeros((d, N_IN_PAD - used), F32)], axis=1).astype(BF16)
    wuq = w_uq[l].reshape(Q_RANK, A_HEADS, A_NOPE + A_ROPE)
    wuq = jnp.pad(wuq, ((0, 0), (0, 0), (0, QK_HEAD - A_NOPE - A_ROPE))).reshape(Q_RANK, A_HEADS * QK_HEAD)
    bias = jnp.zeros((LANE,), F32).at[GATE_I:GATE_I + 2 * M_HEADS].set(b_if[l].astype(F32)).reshape(1, LANE)
    wuk = w_uk[l].astype(BF16)
    return {
        "g_mix": g_mix_norm[l].reshape(1, d), "w_in": w_in_p, "bias": bias,
        "g_mh": g_mh[l].reshape(1, M_WIDTH), "g_cq": g_cq[l].reshape(1, Q_RANK),
        "w_uq": wuq.astype(BF16), "g_ckv": g_ckv[l].reshape(1, KV_RANK),
        "w_uk": wuk, "w_ukt": wuk.T, "w_uv": w_uv[l].astype(BF16),
        "g_qn": g_qn[l].reshape(1, A_NOPE), "g_qr": _pad_lanes(g_qr[l]),
        "g_kn": g_kn[l].reshape(1, A_NOPE), "g_kr": _pad_lanes(g_kr[l]),
        "w_out": w_out[l].astype(BF16), "g_ffn": g_ffn_norm[l].reshape(1, d),
        "g_ple": g_ple_norm[l].reshape(1, d), "w_ple_gate": w_ple_gate[l].astype(BF16),
        "w_ple_proj": w_ple_proj[l].astype(BF16),
    }


def _row_tile(t, pref):
    return pref if t % pref == 0 else t


def kernel(x_prompt, x_sample, p_prompt, p_sample, cache_ckv, cache_kpe, state_C, state_n, state_m, page_table, g_mix_norm, w_in, b_if, g_mh, g_cq, w_uq, g_ckv, w_uk, w_uv, g_qn, g_qr, g_kn, g_kr, w_out, g_ffn_norm, w_gate, w_up, w_down, w_router, w_gate_e, w_up_e, w_down_e, g_ple_norm, w_ple_gate, w_ple_proj):
    depth = w_in.shape[0]
    bp, sp, d = x_prompt.shape
    bs, ss, _ = x_sample.shape
    n_pages = page_table.shape[1]
    past = n_pages * PAGE_SIZE
    tp = bp * sp
    ts = bs * ss
    tm_p = _row_tile(sp, 512)
    tm_s = ts
    chunk = next((c for c in (MLSTM_CHUNK, M_CHUNK) if sp % c == 0), sp)
    pad_s = 16

    cos_p, sin_p = _rope_tables(jnp.arange(sp))
    cos_s, sin_s = _rope_tables(past + jnp.arange(ss))
    cos_s = jnp.broadcast_to(cos_s[None], (bs, ss, LANE)).reshape(ts, LANE)
    sin_s = jnp.broadcast_to(sin_s[None], (bs, ss, LANE)).reshape(ts, LANE)

    cache_kpe_t = jnp.swapaxes(cache_kpe, 2, 3)
    y_p = x_prompt.reshape(tp, d)
    y_s = x_sample.reshape(ts, d)
    outs = {k: [] for k in ("ckv_p", "kpe_p", "C_p", "n_p", "m_p", "ckv_s", "kpe_s", "C_s", "n_s", "m_s")}
    zc = jnp.zeros((bp, M_HEADS, M_DQK, M_DV), F32)
    zn = jnp.zeros((bp, M_HEADS, 1, M_DQK), F32)
    zm = jnp.zeros((bp, M_HEADS, 1, LANE), F32)

    for l in range(depth):
        lw = _layer_weights(l, g_mix_norm, w_in, b_if, g_mh, g_cq, w_uq, g_ckv, w_uk, w_uv, g_qn, g_qr, g_kn,
                            g_kr, w_out, g_ffn_norm, g_ple_norm, w_ple_gate, w_ple_proj)

        z_p = norm_matmul(y_p, lw["g_mix"], lw["w_in"], tm_p, 1024)
        ckv_p, kpe_p, q_p, k_p, v_p = mla_prep(z_p, cos_p, sin_p, lw, tm_p, sp // tm_p, False)
        hm_p, c1, n1, m1 = mlstm(z_p, lw["bias"], lw["g_mh"], zc, zn, zm, bp, sp // chunk, chunk, chunk)
        ha_p = prompt_attention(q_p, k_p, v_p, bp, sp, tm_p)
        y_p = out_proj(hm_p, ha_p, lw["w_out"], y_p, tm_p, 1024)
        outs["ckv_p"].append(ckv_p.reshape(bp, sp, KV_RANK))
        outs["kpe_p"].append(kpe_p.reshape(bp, sp, A_ROPE))
        outs["C_p"].append(c1)
        outs["n_p"].append(n1.reshape(bp, M_HEADS, M_DQK))
        outs["m_p"].append(m1[:, :, 0, 0])

        z_s = norm_matmul(y_s, lw["g_mix"], lw["w_in"], tm_s, 1024)
        ckv_s, kpe_s, q_s, k_s, _, a_s = mla_prep(z_s, cos_s, sin_s, lw, tm_s, 1, True)
        z_s8 = jnp.pad(z_s.reshape(ts, 1, N_IN_PAD), ((0, 0), (pad_s - 1, 0), (0, 0))).reshape(ts * pad_s, N_IN_PAD)
        hm_s8, c2, n2, m2 = mlstm(z_s8, lw["bias"], lw["g_mh"], state_C[l],
                                  state_n[l].reshape(bs, M_HEADS, 1, M_DQK),
                                  jnp.broadcast_to(state_m[l][:, :, None, None], (bs, M_HEADS, 1, LANE)),
                                  ts, 1, pad_s, 1)
        hm_s = hm_s8.reshape(ts, pad_s, M_WIDTH)[:, pad_s - 1, :]
        ha_s = decode_attention(l, page_table, cache_ckv, cache_kpe_t,
                                q_s.reshape(ts, A_HEADS, QK_HEAD), k_s.reshape(ts, A_HEADS, QK_HEAD),
                                a_s.reshape(ts, A_HEADS, KV_RANK), ckv_s.reshape(ts, 1, KV_RANK),
                                lw["w_ukt"], lw["w_uv"], 8 if n_pages % 8 == 0 else 1)
        y_s = out_proj(hm_s, ha_s.reshape(ts, A_WIDTH).astype(BF16), lw["w_out"], y_s, tm_s, 1024)
        outs["ckv_s"].append(ckv_s.reshape(bs, ss, KV_RANK))
        outs["kpe_s"].append(kpe_s.reshape(bs, ss, A_ROPE))
        outs["C_s"].append(c2)
        outs["n_s"].append(n2.reshape(bs, M_HEADS, M_DQK))
        outs["m_s"].append(m2[:, :, 0, 0])

        if l % 2 == 0:
            wg, wu, wd = (w[l // 2].astype(BF16) for w in (w_gate, w_up, w_down))
            tf = 512 if wg.shape[1] % 512 == 0 else wg.shape[1]
            y_p = ffn_dense(y_p, lw["g_ffn"], wg, wu, wd, tm_p, tf)
            y_s = ffn_dense(y_s, lw["g_ffn"], wg, wu, wd, tm_s, tf)
        else:
            wg, wu, wd = (w[l // 2].astype(BF16) for w in (w_gate_e, w_up_e, w_down_e))
            wr = jnp.pad(w_router[l // 2], ((0, 0), (0, LANE - N_EXPERTS))).astype(BF16)
            tf = 512 if wg.shape[2] % 512 == 0 else wg.shape[2]
            y_p = moe_mixer(y_p, lw["g_ffn"], wr, wg, wu, wd, min(tm_p, 256), tm_p, tf)
            y_s = moe_mixer(y_s, lw["g_ffn"], wr, wg, wu, wd, tm_s, 128, tf)
        y_p = ple(y_p, lw["g_ple"], p_prompt[l].reshape(tp, -1), lw["w_ple_gate"], lw["w_ple_proj"], tm_p, 1024)
        y_s = ple(y_s, lw["g_ple"], p_sample[l].reshape(ts, -1), lw["w_ple_gate"], lw["w_ple_proj"], tm_s, 1024)

    st = lambda k: jnp.stack(outs[k])
    return (y_p.reshape(bp, sp, d), y_s.reshape(bs, ss, d),
            st("ckv_p"), st("kpe_p"), st("C_p"), st("n_p"), st("m_p"),
            st("ckv_s"), st("kpe_s"), st("C_s"), st("n_s"), st("m_s"))
```

```python
import functools

import jax
import jax.numpy as jnp
from jax import lax
from jax.experimental import pallas as pl
from jax.experimental.pallas import tpu as pltpu

F32 = jnp.float32
BF16 = jnp.bfloat16

M_HEADS = 4
M_DQK = 128
M_DV = 256
M_WIDTH = M_HEADS * M_DV
M_CHUNK = 64
MLSTM_CHUNK = 256
A_HEADS = 8
A_NOPE = 128
A_ROPE = 64
A_VD = 128
A_WIDTH = A_HEADS * A_VD
Q_RANK = 512
KV_RANK = 256
ROPE_THETA = 10000.0
MLA_SCALE = (A_NOPE + A_ROPE) ** -0.5
LOG2_E = 1.4426950408889634
PAGE_SIZE = 128
N_EXPERTS = 8
EPS = 1e-6

LANE = 128
MIB = 2 ** 20

COL_MQ = 0
COL_MK = COL_MQ + M_HEADS * M_DQK
COL_MV = COL_MK + M_HEADS * M_DQK
COL_MO = COL_MV + M_WIDTH
COL_CQ = COL_MO + M_WIDTH
COL_CKV = COL_CQ + Q_RANK
COL_KG = COL_CKV + KV_RANK
N_IN_PAD = 4096
GATE_I = A_ROPE
GATE_F = A_ROPE + M_HEADS
QK_HEAD = 2 * LANE
NEG_BIG = -1e30


def _cparams(sem, vmem_mib):
    return pltpu.CompilerParams(dimension_semantics=sem, vmem_limit_bytes=vmem_mib * MIB)


def _rms(x, g, n):
    ms = jnp.sum(x * x, axis=-1, keepdims=True) * (1.0 / n)
    return x * lax.rsqrt(ms + EPS) * g


def _dot(a, b):
    return jnp.dot(a, b, preferred_element_type=F32)


def _dot_nt(a, b):
    return lax.dot_general(a, b, (((1,), (1,)), ((), ())), preferred_element_type=F32)


def _dot_tn(a, b):
    return lax.dot_general(a, b, (((0,), (0,)), ((), ())), preferred_element_type=F32)


def _norm_matmul_kernel(x_ref, g_ref, w_ref, o_ref, xn_ref):
    @pl.when(pl.program_id(1) == 0)
    def _():
        xn_ref[...] = _rms(x_ref[...], g_ref[...], x_ref.shape[-1]).astype(BF16)

    o_ref[...] = _dot(xn_ref[...], w_ref[...])


def norm_matmul(x, g, w, tm, tn):
    t, d = x.shape
    n = w.shape[1]
    return pl.pallas_call(
        _norm_matmul_kernel,
        grid=(t // tm, n // tn),
        in_specs=[pl.BlockSpec((tm, d), lambda i, j: (i, 0)),
                  pl.BlockSpec((1, d), lambda i, j: (0, 0)),
                  pl.BlockSpec((d, tn), lambda i, j: (0, j))],
        out_specs=pl.BlockSpec((tm, tn), lambda i, j: (i, j)),
        out_shape=jax.ShapeDtypeStruct((t, n), F32),
        scratch_shapes=[pltpu.VMEM((tm, d), BF16)],
        compiler_params=_cparams(("parallel", "arbitrary"), 40),
        name="norm_matmul",
    )(x, g, w)


def _mla_prep_kernel(cq_ref, ckv_ref, kg_ref, cos_ref, sin_ref, wuq_ref, wuk_ref, wuv_ref,
                     gcq_ref, gckv_ref, gkr_ref, gqn_ref, gqr_ref, gkn_ref,
                     ckv_out, kpe_out, q_out, k_out, v_out, *a_out, absorb):
    cos = cos_ref[...]
    sin = sin_ref[...]

    def rope(x):
        return x * cos + (pltpu.roll(x, A_ROPE // 2, 1) + pltpu.roll(x, LANE - A_ROPE // 2, 1)) * sin

    ckv = _rms(ckv_ref[...], gckv_ref[...], KV_RANK)
    ckv_out[...] = ckv
    ckv_b = ckv.astype(BF16)

    kg = kg_ref[...]
    lane = lax.broadcasted_iota(jnp.int32, kg.shape, 1)
    kp = jnp.where(lane < A_ROPE, kg, 0.0)
    kpr = rope(_rms(kp, gkr_ref[...], A_ROPE))
    kpe_out[...] = kpr[:, :A_ROPE]
    kpr_b = kpr.astype(BF16)

    kn = _dot(ckv_b, wuk_ref[...])
    v_out[...] = _dot(ckv_b, wuv_ref[...]).astype(BF16)
    cq = _rms(cq_ref[...], gcq_ref[...], Q_RANK).astype(BF16)
    qf = _dot(cq, wuq_ref[...])
    gkn = gkn_ref[...]
    for h in range(A_HEADS):
        n0 = h * A_NOPE
        c0 = h * QK_HEAD
        k_out[:, c0:c0 + LANE] = _rms(kn[:, n0:n0 + A_NOPE], gkn, A_NOPE).astype(BF16)
        k_out[:, c0 + LANE:c0 + QK_HEAD] = kpr_b
        qn = _rms(qf[:, c0:c0 + LANE], gqn_ref[...], A_NOPE)
        q_out[:, c0:c0 + LANE] = qn.astype(BF16)
        qp = _rms(qf[:, c0 + LANE:c0 + QK_HEAD], gqr_ref[...], A_ROPE)
        q_out[:, c0 + LANE:c0 + QK_HEAD] = rope(qp).astype(BF16)
        if absorb:
            a_out[0][:, h * KV_RANK:(h + 1) * KV_RANK] = _dot_nt(
                (qn * gkn).astype(BF16), wuk_ref[:, n0:n0 + A_NOPE])


def mla_prep(z, cos, sin, lw, tm, n_pos_blocks, absorb):
    t = z.shape[0]
    row = lambda i: (i, 0)
    full = lambda i: (0, 0)
    out_shape = [jax.ShapeDtypeStruct((t, KV_RANK), F32),
                 jax.ShapeDtypeStruct((t, A_ROPE), F32),
                 jax.ShapeDtypeStruct((t, A_HEADS * QK_HEAD), BF16),
                 jax.ShapeDtypeStruct((t, A_HEADS * QK_HEAD), BF16),
                 jax.ShapeDtypeStruct((t, A_WIDTH), BF16)]
    out_specs = [pl.BlockSpec((tm, KV_RANK), row), pl.BlockSpec((tm, A_ROPE), row),
                 pl.BlockSpec((tm, A_HEADS * QK_HEAD), row), pl.BlockSpec((tm, A_HEADS * QK_HEAD), row),
                 pl.BlockSpec((tm, A_WIDTH), row)]
    if absorb:
        out_shape.append(jax.ShapeDtypeStruct((t, A_HEADS * KV_RANK), F32))
        out_specs.append(pl.BlockSpec((tm, A_HEADS * KV_RANK), row))
    return pl.pallas_call(
        functools.partial(_mla_prep_kernel, absorb=absorb),
        grid=(t // tm,),
        in_specs=[pl.BlockSpec((tm, Q_RANK), lambda i: (i, COL_CQ // Q_RANK)),
                  pl.BlockSpec((tm, KV_RANK), lambda i: (i, COL_CKV // KV_RANK)),
                  pl.BlockSpec((tm, LANE), lambda i: (i, COL_KG // LANE)),
                  pl.BlockSpec((tm, LANE), lambda i: (i % n_pos_blocks, 0)),
                  pl.BlockSpec((tm, LANE), lambda i: (i % n_pos_blocks, 0)),
                  pl.BlockSpec((Q_RANK, A_HEADS * QK_HEAD), full),
                  pl.BlockSpec((KV_RANK, A_HEADS * A_NOPE), full),
                  pl.BlockSpec((KV_RANK, A_WIDTH), full),
                  pl.BlockSpec((1, Q_RANK), full), pl.BlockSpec((1, KV_RANK), full),
                  pl.BlockSpec((1, LANE), full), pl.BlockSpec((1, LANE), full),
                  pl.BlockSpec((1, LANE), full), pl.BlockSpec((1, LANE), full)],
        out_specs=out_specs,
        out_shape=out_shape,
        compiler_params=_cparams(("parallel",), 48),
        name="mla_prep",
    )(z, z, z, cos, sin, lw["w_uq"], lw["w_uk"], lw["w_uv"], lw["g_cq"], lw["g_ckv"],
      lw["g_kr"], lw["g_qn"], lw["g_qr"], lw["g_kn"])


def _mlstm_kernel(q_ref, k_ref, v_ref, o_ref, kg_ref, bias_ref, gmh_ref, c0_ref, n0_ref, m0_ref,
                  h_out, c_out, n_out, m_out, c_s, n_s, m_s, *, chunk, n_real, group):
    c = pl.program_id(1)

    @pl.when(c == 0)
    def _():
        c_s[...] = c0_ref[...]
        n_s[...] = n0_ref[...]
        m_s[...] = m0_ref[...]

    L = chunk
    gates_all = kg_ref[...] + bias_ref[...]
    lf_all_rows = jax.nn.log_sigmoid(gates_all)
    ri = lax.broadcasted_iota(jnp.int32, (L, L), 0)
    ci = lax.broadcasted_iota(jnp.int32, (L, L), 1)
    tri = ci <= ri
    eye = ci == ri
    rowid = lax.broadcasted_iota(jnp.int32, (L, 1), 0)
    real = rowid >= (L - n_real)
    scale = M_DQK ** -0.5

    def to_row(col):
        return jnp.sum(jnp.where(eye, col, 0.0), axis=0, keepdims=True)

    for sh in range(group * M_HEADS):
        s, h = divmod(sh, M_HEADS)
        r0 = s * L
        i_col = gates_all[r0:r0 + L, GATE_I + h:GATE_I + h + 1]
        lf_col = lf_all_rows[r0:r0 + L, GATE_F + h:GATE_F + h + 1]
        if n_real < L:
            i_col = jnp.where(real, i_col, NEG_BIG)
            lf_col = jnp.where(real, lf_col, 0.0)
        i_row = to_row(i_col)
        lf_row = to_row(lf_col)
        bh_col = jnp.sum(jnp.where(tri, lf_row, 0.0), axis=1, keepdims=True)
        bh_row = to_row(bh_col)
        dmat = jnp.where(tri, bh_col - bh_row + i_row, -jnp.inf)
        m_prev = m_s[s, h][:, 0:1]
        inter = bh_col + m_prev
        m_t = jnp.maximum(inter, jnp.max(dmat, axis=1, keepdims=True))
        w_inter = jnp.exp(inter - m_t)
        qh = q_ref[r0:r0 + L, h * M_DQK:(h + 1) * M_DQK]
        kh = k_ref[r0:r0 + L, h * M_DQK:(h + 1) * M_DQK]
        vh = v_ref[r0:r0 + L, h * M_DV:(h + 1) * M_DV]
        qb = qh.astype(BF16)
        vb = vh.astype(BF16)
        s_qk = _dot_nt(qb, kh.astype(BF16)) * scale * jnp.exp(dmat - m_t)
        c_prev = c_s[s, h]
        n_prev = n_s[s, h]
        num = _dot(s_qk.astype(BF16), vb) + _dot(qb, c_prev.astype(BF16)) * (scale * w_inter)
        qn = jnp.sum(qh * n_prev, axis=1, keepdims=True) * scale
        den = jnp.sum(s_qk, axis=1, keepdims=True) + w_inter * qn
        den = jnp.maximum(jnp.abs(den), jnp.exp(-m_t))
        hh = num / den
        m_last = m_t[L - 1:L, :]
        bh_last = bh_col[L - 1:L, :]
        g_inter = jnp.exp(bh_last + m_prev - m_last)
        g_intra = jnp.exp(bh_last - bh_col + i_col - m_last)
        kw = kh * g_intra
        c_s[s, h] = g_inter * c_prev + _dot_tn(kw.astype(BF16), vb)
        n_s[s, h] = g_inter * n_prev + jnp.sum(kw, axis=0, keepdims=True)
        m_s[s, h] = jnp.broadcast_to(m_last, (1, LANE))
        hn = _rms(hh, gmh_ref[:, h * M_DV:(h + 1) * M_DV], M_DV)
        og = o_ref[r0:r0 + L, h * M_DV:(h + 1) * M_DV]
        h_out[r0:r0 + L, h * M_DV:(h + 1) * M_DV] = (hn * jax.nn.sigmoid(og)).astype(BF16)

    @pl.when(c == pl.num_programs(1) - 1)
    def _():
        c_out[...] = c_s[...]
        n_out[...] = n_s[...]
        m_out[...] = m_s[...]


def mlstm(z, bias, gmh, c0_all, n0, m0, layer, batch, n_chunks, chunk, n_real, group):
    assert group == 1 or n_chunks == 1
    t = z.shape[0]
    rows = group * chunk
    blk = lambda w, col: pl.BlockSpec((rows, w), lambda b, c: (b * n_chunks + c, col))
    st = lambda shape: pl.BlockSpec((group,) + shape[1:], lambda b, c: (b, 0, 0, 0))
    c_shape = c0_all.shape[1:]
    full = lambda b, c: (0, 0)
    return pl.pallas_call(
        functools.partial(_mlstm_kernel, chunk=chunk, n_real=n_real, group=group),
        grid=(batch // group, n_chunks),
        in_specs=[blk(M_HEADS * M_DQK, COL_MQ // (M_HEADS * M_DQK)),
                  blk(M_HEADS * M_DQK, COL_MK // (M_HEADS * M_DQK)),
                  blk(M_WIDTH, COL_MV // M_WIDTH), blk(M_WIDTH, COL_MO // M_WIDTH),
                  blk(LANE, COL_KG // LANE),
                  pl.BlockSpec((1, LANE), full), pl.BlockSpec((1, M_WIDTH), full),
                  pl.BlockSpec((None, group) + c_shape[1:], lambda b, c: (layer, b, 0, 0, 0)),
                  st(n0.shape), st(m0.shape)],
        out_specs=[pl.BlockSpec((rows, M_WIDTH), lambda b, c: (b * n_chunks + c, 0)),
                   st(c_shape), st(n0.shape), st(m0.shape)],
        out_shape=[jax.ShapeDtypeStruct((t, M_WIDTH), BF16),
                   jax.ShapeDtypeStruct(c_shape, F32), jax.ShapeDtypeStruct(n0.shape, F32),
                   jax.ShapeDtypeStruct(m0.shape, F32)],
        scratch_shapes=[pltpu.VMEM((group,) + c_shape[1:], F32), pltpu.VMEM((group,) + n0.shape[1:], F32),
                        pltpu.VMEM((group,) + m0.shape[1:], F32)],
        compiler_params=_cparams(("parallel", "arbitrary"), 32),
        name="mlstm",
    )(z, z, z, z, z, bias, gmh, c0_all, n0, m0)


def _attn_kernel(q_ref, k_ref, v_ref, o_ref, *, tq, nq):
    qi = pl.program_id(2)
    q = q_ref[...]
    c2 = MLA_SCALE * LOG2_E
    row = lax.broadcasted_iota(jnp.int32, (tq, tq), 0)
    col = lax.broadcasted_iota(jnp.int32, (tq, tq), 1)

    for n in range(nq):
        @pl.when(qi == n)
        def _(n=n):
            lo = n * tq
            for hh in range(ATTN_HEADS_PER_STEP):
                qc = slice(hh * QK_HEAD, (hh + 1) * QK_HEAD)
                vc = slice(hh * A_VD, (hh + 1) * A_VD)
                qh = q[:, qc]
                s_d = jnp.where(col <= row, _dot_nt(qh, k_ref[lo:lo + tq, qc]) * c2, -jnp.inf)
                m = jnp.max(s_d, axis=1, keepdims=True)
                if n:
                    s_o = _dot_nt(qh, k_ref[0:lo, qc]) * c2
                    m = jnp.maximum(m, jnp.max(s_o, axis=1, keepdims=True))
                p_d = jnp.exp2(s_d - m)
                l = jnp.sum(p_d, axis=1, keepdims=True)
                acc = _dot(p_d.astype(BF16), v_ref[lo:lo + tq, vc])
                if n:
                    p_o = jnp.exp2(s_o - m)
                    l = l + jnp.sum(p_o, axis=1, keepdims=True)
                    acc = acc + _dot(p_o.astype(BF16), v_ref[0:lo, vc])
                o_ref[:, vc] = (acc / l).astype(o_ref.dtype)


ATTN_HEADS_PER_STEP = 4


def prompt_attention(q, k, v, batch, seq, tq):
    nq = seq // tq
    hps = ATTN_HEADS_PER_STEP
    return pl.pallas_call(
        functools.partial(_attn_kernel, tq=tq, nq=nq),
        grid=(batch, A_HEADS // hps, nq),
        in_specs=[pl.BlockSpec((tq, hps * QK_HEAD), lambda b, h, i: (b * nq + i, h)),
                  pl.BlockSpec((seq, hps * QK_HEAD), lambda b, h, i: (b, h)),
                  pl.BlockSpec((seq, hps * A_VD), lambda b, h, i: (b, h))],
        out_specs=pl.BlockSpec((tq, hps * A_VD), lambda b, h, i: (b * nq + i, h)),
        out_shape=jax.ShapeDtypeStruct((batch * seq, A_WIDTH), BF16),
        compiler_params=_cparams(("parallel", "parallel", "arbitrary"), 32),
        name="prompt_attention",
    )(q, k, v)


DECODE_SLOTS = 3


def _decode_attn_kernel(pt_ref, q_ref, knew_ref, a_ref, cnew_ref, wukt_ref, wuv_ref, ckv_hbm, kpe_hbm, o_ref,
                        lhs_s, cbuf, kbuf, cb_s, sem_c, sem_k, *, layer, pps, n_chunks):
    b = pl.program_id(0)
    nb = pl.num_programs(0)
    total = nb * n_chunks
    n_up = A_HEADS * A_NOPE
    tc = pps * PAGE_SIZE

    def start_chunk(g):
        slot = g % DECODE_SLOTS
        gg = jnp.minimum(g, total - 1)
        gb = gg // n_chunks
        gc = gg % n_chunks
        for p in range(pps):
            page = pt_ref[gb, gc * pps + p]
            pltpu.make_async_copy(ckv_hbm.at[layer, page], cbuf.at[slot, p], sem_c.at[slot]).start()
            pltpu.make_async_copy(kpe_hbm.at[layer, page], kbuf.at[slot, p], sem_k.at[slot]).start()

    def wait_chunk(slot):
        pltpu.make_async_copy(ckv_hbm.at[layer, pl.ds(0, pps)], cbuf.at[slot], sem_c.at[slot]).wait()
        pltpu.make_async_copy(kpe_hbm.at[layer, pl.ds(0, pps)], kbuf.at[slot], sem_k.at[slot]).wait()

    @pl.when(b == 0)
    def _():
        lhs_s[0:n_up, :] = wukt_ref[...]
        start_chunk(0)
        start_chunk(1)

    lhs_s[n_up:, :] = jnp.concatenate(
        [a_ref[...], jnp.zeros((lhs_s.shape[0] - n_up - A_HEADS, KV_RANK), F32)], axis=0).astype(BF16)
    q = q_ref[...]
    qpe = q[:, LANE:LANE + A_ROPE]
    s_new = jnp.sum(q.astype(F32) * knew_ref[...].astype(F32), axis=1, keepdims=True) * MLA_SCALE
    hid = lax.broadcasted_iota(jnp.int32, (A_HEADS, tc), 0)

    def scores(c):
        g = b * n_chunks + c
        start_chunk(g + 2)
        slot = g % DECODE_SLOTS
        wait_chunk(slot)
        cb = cbuf[slot].reshape(tc, KV_RANK).astype(BF16)
        cb_s[c % 2] = cb
        up = _dot_nt(lhs_s[...], cb)
        ss = jnp.zeros((A_HEADS, tc), F32)
        for h in range(A_HEADS):
            blk = up[h * A_NOPE:(h + 1) * A_NOPE, :]
            ss = jnp.where(hid == h, jnp.sum(blk * blk, axis=0, keepdims=True), ss)
        s_nope = up[n_up:n_up + A_HEADS, :] * lax.rsqrt(ss * (1.0 / A_NOPE) + EPS)
        s_pe = jnp.concatenate([_dot(qpe, kbuf[slot, p].astype(BF16)) for p in range(pps)], axis=1)
        return (s_nope + s_pe) * MLA_SCALE

    def accumulate(c, s, state):
        m_old, l_old, acc_old = state
        m_new = jnp.maximum(m_old, jnp.max(s, axis=1, keepdims=True))
        alpha = jnp.exp(m_old - m_new)
        p_att = jnp.exp(s - m_new)
        l_new = alpha * l_old + jnp.sum(p_att, axis=1, keepdims=True)
        acc_new = alpha * acc_old + _dot(p_att.astype(BF16), cb_s[c % 2])
        return m_new, l_new, acc_new

    def body(c, carry):
        s_cur = scores(c)
        return (s_cur,) + accumulate(c - 1, carry[0], carry[1:])

    init = (s_new, jnp.ones((A_HEADS, 1), F32), jnp.broadcast_to(cnew_ref[...], (A_HEADS, KV_RANK)))
    carry = lax.fori_loop(1, n_chunks, body, (scores(0),) + init)
    _, l_fin, acc_fin = accumulate(n_chunks - 1, carry[0], carry[1:])

    o_lat = acc_fin / l_fin
    full = _dot(o_lat.astype(BF16), wuv_ref[...])
    hid_o = lax.broadcasted_iota(jnp.int32, (A_HEADS, A_VD), 0)
    res = jnp.zeros((A_HEADS, A_VD), F32)
    for h in range(A_HEADS):
        res = jnp.where(hid_o == h, full[:, h * A_VD:(h + 1) * A_VD], res)
    o_ref[...] = res

    @pl.when(b == nb - 1)
    def _():
        wait_chunk(total % DECODE_SLOTS)
        wait_chunk((total + 1) % DECODE_SLOTS)


def decode_attention(layer, page_table, cache_ckv, cache_kpe_t, q3, k3, a3, cnew3, wukt, wuv, pps):
    nb, n_pages = page_table.shape
    n_up = A_HEADS * A_NOPE
    per_seq = lambda rows, w: pl.BlockSpec((None, rows, w), lambda b, pt: (b, 0, 0))
    full = lambda b, pt: (0, 0)
    return pl.pallas_call(
        functools.partial(_decode_attn_kernel, layer=layer, pps=pps, n_chunks=n_pages // pps),
        grid_spec=pltpu.PrefetchScalarGridSpec(
            num_scalar_prefetch=1,
            grid=(nb,),
            in_specs=[per_seq(A_HEADS, QK_HEAD), per_seq(A_HEADS, QK_HEAD), per_seq(A_HEADS, KV_RANK),
                      per_seq(1, KV_RANK),
                      pl.BlockSpec((n_up, KV_RANK), full), pl.BlockSpec((KV_RANK, A_WIDTH), full),
                      pl.BlockSpec(memory_space=pl.ANY), pl.BlockSpec(memory_space=pl.ANY)],
            out_specs=pl.BlockSpec((None, A_HEADS, A_VD), lambda b, pt: (b, 0, 0)),
            scratch_shapes=[pltpu.VMEM((n_up + 16, KV_RANK), BF16),
                            pltpu.VMEM((DECODE_SLOTS, pps, PAGE_SIZE, KV_RANK), F32),
                            pltpu.VMEM((DECODE_SLOTS, pps, A_ROPE, PAGE_SIZE), F32),
                            pltpu.VMEM((2, pps * PAGE_SIZE, KV_RANK), BF16),
                            pltpu.SemaphoreType.DMA((DECODE_SLOTS,)),
                            pltpu.SemaphoreType.DMA((DECODE_SLOTS,))]),
        out_shape=jax.ShapeDtypeStruct((nb, A_HEADS, A_VD), F32),
        compiler_params=_cparams(("arbitrary",), 48),
        name="decode_attention",
    )(page_table, q3, k3, a3, cnew3, wukt, wuv, cache_ckv, cache_kpe_t)


def _out_proj_kernel(hm_ref, ha_ref, w_ref, x_ref, o_ref):
    acc = _dot(hm_ref[...], w_ref[0:M_WIDTH, :]) + _dot(ha_ref[...], w_ref[M_WIDTH:, :])
    o_ref[...] = x_ref[...] + acc


def out_proj(hm, ha, w, x, tm, tn):
    t, d = x.shape
    return pl.pallas_call(
        _out_proj_kernel,
        grid=(t // tm, d // tn),
        in_specs=[pl.BlockSpec((tm, M_WIDTH), lambda i, j: (i, 0)),
                  pl.BlockSpec((tm, A_WIDTH), lambda i, j: (i, 0)),
                  pl.BlockSpec((M_WIDTH + A_WIDTH, tn), lambda i, j: (0, j)),
                  pl.BlockSpec((tm, tn), lambda i, j: (i, j))],
        out_specs=pl.BlockSpec((tm, tn), lambda i, j: (i, j)),
        out_shape=jax.ShapeDtypeStruct((t, d), F32),
        compiler_params=_cparams(("parallel", "arbitrary"), 40),
        name="out_proj",
    )(hm, ha, w, x)


def _ffn_kernel(x_ref, g_ref, wg_ref, wu_ref, wd_ref, o_ref, xn_ref, acc_ref):
    f = pl.program_id(1)

    @pl.when(f == 0)
    def _():
        xn_ref[...] = _rms(x_ref[...], g_ref[...], x_ref.shape[-1]).astype(BF16)
        acc_ref[...] = jnp.zeros(acc_ref.shape, F32)

    xn = xn_ref[...]
    a = jax.nn.silu(_dot(xn, wg_ref[...])) * _dot(xn, wu_ref[...])
    acc_ref[...] += _dot(a.astype(BF16), wd_ref[...])

    @pl.when(f == pl.num_programs(1) - 1)
    def _():
        o_ref[...] = x_ref[...] + acc_ref[...]


def ffn_dense(x, g, wg, wu, wd, tm, tf):
    t, d = x.shape
    ff = wg.shape[1]
    return pl.pallas_call(
        _ffn_kernel,
        grid=(t // tm, ff // tf),
        in_specs=[pl.BlockSpec((tm, d), lambda i, f: (i, 0)),
                  pl.BlockSpec((1, d), lambda i, f: (0, 0)),
                  pl.BlockSpec((d, tf), lambda i, f: (0, f)),
                  pl.BlockSpec((d, tf), lambda i, f: (0, f)),
                  pl.BlockSpec((tf, d), lambda i, f: (f, 0))],
        out_specs=pl.BlockSpec((tm, d), lambda i, f: (i, 0)),
        out_shape=jax.ShapeDtypeStruct((t, d), F32),
        scratch_shapes=[pltpu.VMEM((tm, d), BF16), pltpu.VMEM((tm, d), F32)],
        compiler_params=_cparams(("parallel", "arbitrary"), 48),
        name="ffn_dense",
    )(x, g, wg, wu, wd)


def _ple_kernel(x_ref, g_ref, p_ref, wg_ref, wp_ref, o_ref, xn_ref, *, tn):
    j = pl.program_id(1)

    @pl.when(j == 0)
    def _():
        xn_ref[...] = _rms(x_ref[...], g_ref[...], x_ref.shape[-1]).astype(BF16)

    gate = jax.nn.sigmoid(_dot(xn_ref[...], wg_ref[...]))
    proj = _dot(p_ref[...].astype(BF16), wp_ref[...])
    o_ref[...] = x_ref[:, pl.ds(pl.multiple_of(j * tn, tn), tn)] + gate * proj


def ple(x, g, p, wg, wp, tm, tn):
    t, d = x.shape
    pd = p.shape[1]
    return pl.pallas_call(
        functools.partial(_ple_kernel, tn=tn),
        grid=(t // tm, d // tn),
        in_specs=[pl.BlockSpec((tm, d), lambda i, j: (i, 0)),
                  pl.BlockSpec((1, d), lambda i, j: (0, 0)),
                  pl.BlockSpec((tm, pd), lambda i, j: (i, 0)),
                  pl.BlockSpec((d, tn), lambda i, j: (0, j)),
                  pl.BlockSpec((pd, tn), lambda i, j: (0, j))],
        out_specs=pl.BlockSpec((tm, tn), lambda i, j: (i, j)),
        out_shape=jax.ShapeDtypeStruct((t, d), F32),
        scratch_shapes=[pltpu.VMEM((tm, d), BF16)],
        compiler_params=_cparams(("parallel", "arbitrary"), 40),
        name="ple",
    )(x, g, p, wg, wp)


def _router_kernel(x_ref, g_ref, wr_ref, idx_out, gate_out, cnt_out, carry_s):
    i = pl.program_id(0)

    @pl.when(i == 0)
    def _():
        carry_s[...] = jnp.zeros(carry_s.shape, F32)

    hn = _rms(x_ref[...], g_ref[...], x_ref.shape[-1])
    logits = _dot(hn.astype(BF16), wr_ref[...])
    tm = logits.shape[0]
    lane = lax.broadcasted_iota(jnp.int32, logits.shape, 1)
    lg = jnp.where(lane < N_EXPERTS, logits, -jnp.inf)
    m1 = jnp.max(lg, axis=1, keepdims=True)
    i1 = jnp.min(jnp.where(lg == m1, lane, LANE), axis=1, keepdims=True)
    lg2 = jnp.where(lane == i1, -jnp.inf, lg)
    m2 = jnp.max(lg2, axis=1, keepdims=True)
    i2 = jnp.min(jnp.where(lg2 == m2, lane, LANE), axis=1, keepdims=True)
    e2 = jnp.exp(m2 - m1)
    g1 = 1.0 / (1.0 + e2)
    g2 = e2 / (1.0 + e2)
    hit1 = lane == i1
    hit2 = lane == i2
    onehot = jnp.where(hit1 | hit2, 1.0, 0.0)
    ri = lax.broadcasted_iota(jnp.int32, (tm, tm), 0)
    ci = lax.broadcasted_iota(jnp.int32, (tm, tm), 1)
    before = jnp.where(ci < ri, 1.0, 0.0).astype(BF16)
    seen = carry_s[...] + _dot(before, onehot.astype(BF16))
    r1 = jnp.sum(jnp.where(hit1, seen, 0.0), axis=1, keepdims=True)
    r2 = jnp.sum(jnp.where(hit2, seen, 0.0), axis=1, keepdims=True)
    carry_s[...] += jnp.sum(onehot, axis=0, keepdims=True)
    idx = jnp.where(lane == 0, i1, jnp.where(lane == 1, i2, 0))
    rank = jnp.where(lane == 2, r1, jnp.where(lane == 3, r2, 0.0)).astype(jnp.int32)
    idx_out[...] = idx + rank
    gate_out[...] = jnp.where(lane == 0, g1, jnp.where(lane == 1, g2, 0.0))
    cnt_out[...] = carry_s[...]


def moe_router(x, g, wr, tm):
    t, d = x.shape
    row = lambda i: (i, 0)
    full = lambda i: (0, 0)
    return pl.pallas_call(
        _router_kernel,
        grid=(t // tm,),
        in_specs=[pl.BlockSpec((tm, d), row), pl.BlockSpec((1, d), full), pl.BlockSpec((d, LANE), full)],
        out_specs=[pl.BlockSpec((tm, LANE), row), pl.BlockSpec((tm, LANE), row),
                   pl.BlockSpec((1, LANE), full)],
        out_shape=[jax.ShapeDtypeStruct((t, LANE), jnp.int32),
                   jax.ShapeDtypeStruct((t, LANE), F32), jax.ShapeDtypeStruct((1, LANE), F32)],
        scratch_shapes=[pltpu.VMEM((1, LANE), F32)],
        compiler_params=_cparams(("arbitrary",), 40),
        name="moe_router",
    )(x, g, wr)


def _dispatch_kernel(dest_ref, x_ref, init_ref, xs_ref, sem, *, tm):
    del init_ref
    i = pl.program_id(0)

    def row_copy(r, k):
        slot = dest_ref[2 * (i * tm + r) + k]
        return pltpu.make_async_copy(x_ref.at[pl.ds(r, 1), :], xs_ref.at[pl.ds(slot, 1), :], sem)

    def start(r, carry):
        row_copy(r, 0).start()
        row_copy(r, 1).start()
        return carry

    def wait(r, carry):
        row_copy(r, 0).wait()
        row_copy(r, 1).wait()
        return carry

    lax.fori_loop(0, tm, start, 0)
    lax.fori_loop(0, tm, wait, 0)


def moe_dispatch(dest_flat, x, n_slots, tm):
    t, d = x.shape
    init = jnp.zeros((n_slots, d), F32)
    return pl.pallas_call(
        functools.partial(_dispatch_kernel, tm=tm),
        grid_spec=pltpu.PrefetchScalarGridSpec(
            num_scalar_prefetch=1,
            grid=(t // tm,),
            in_specs=[pl.BlockSpec((tm, d), lambda i, dr: (i, 0)), pl.BlockSpec(memory_space=pl.ANY)],
            out_specs=pl.BlockSpec(memory_space=pl.ANY),
            scratch_shapes=[pltpu.SemaphoreType.DMA(())]),
        out_shape=jax.ShapeDtypeStruct((n_slots, d), F32),
        input_output_aliases={2: 0},
        compiler_params=_cparams(("arbitrary",), 32),
        name="moe_dispatch",
    )(dest_flat, x, init)


def _expert_ffn_kernel(be_ref, nu_ref, x_ref, g_ref, wg_ref, wu_ref, wd_ref, o_ref, xn_ref, acc_ref):
    del be_ref
    i = pl.program_id(0)
    f = pl.program_id(1)
    used = i < nu_ref[0]

    @pl.when(f == 0)
    def _():
        xn_ref[...] = _rms(x_ref[...], g_ref[...], x_ref.shape[-1]).astype(BF16)
        acc_ref[...] = jnp.zeros(acc_ref.shape, F32)

    @pl.when(used)
    def _():
        x = xn_ref[...]
        a = jax.nn.silu(_dot(x, wg_ref[...])) * _dot(x, wu_ref[...])
        acc_ref[...] += _dot(a.astype(BF16), wd_ref[...])

    @pl.when(f == pl.num_programs(1) - 1)
    def _():
        o_ref[...] = acc_ref[...]


def expert_ffn(blk_e, n_used, xs, g, wg, wu, wd, blk, tf):
    p, d = xs.shape
    ff = wg.shape[2]
    nf = ff // tf

    def wcol(i, f, be, nu):
        live = i < nu[0]
        return (be[i], 0, jnp.where(live, f, nf - 1))

    def wrow(i, f, be, nu):
        live = i < nu[0]
        return (be[i], jnp.where(live, f, nf - 1), 0)

    return pl.pallas_call(
        _expert_ffn_kernel,
        grid_spec=pltpu.PrefetchScalarGridSpec(
            num_scalar_prefetch=2,
            grid=(p // blk, nf),
            in_specs=[pl.BlockSpec((blk, d), lambda i, f, be, nu: (i, 0)),
                      pl.BlockSpec((1, d), lambda i, f, be, nu: (0, 0)),
                      pl.BlockSpec((None, d, tf), wcol), pl.BlockSpec((None, d, tf), wcol),
                      pl.BlockSpec((None, tf, d), wrow)],
            out_specs=pl.BlockSpec((blk, d), lambda i, f, be, nu: (i, 0)),
            scratch_shapes=[pltpu.VMEM((blk, d), BF16), pltpu.VMEM((blk, d), F32)]),
        out_shape=jax.ShapeDtypeStruct((p, d), F32),
        compiler_params=_cparams(("arbitrary", "arbitrary"), 48),
        name="expert_ffn",
    )(blk_e, n_used, xs, g, wg, wu, wd)


def _combine_kernel(dest_ref, x_ref, gate_ref, ys_ref, o_ref, buf, sem):
    i = pl.program_id(0)
    n = pl.num_programs(0)
    tm = x_ref.shape[0]

    def row_copy(step, slot, r, k):
        src = dest_ref[2 * (step * tm + r) + k]
        return pltpu.make_async_copy(ys_ref.at[pl.ds(src, 1), :], buf.at[slot, k, pl.ds(r, 1), :], sem.at[slot])

    def start_block(step, slot):
        def body(r, carry):
            row_copy(step, slot, r, 0).start()
            row_copy(step, slot, r, 1).start()
            return carry
        lax.fori_loop(0, tm, body, 0)

    @pl.when(i == 0)
    def _():
        start_block(0, 0)

    @pl.when(i + 1 < n)
    def _():
        start_block(i + 1, (i + 1) % 2)

    slot = i % 2

    def wait_body(r, carry):
        row_copy(i, slot, r, 0).wait()
        row_copy(i, slot, r, 1).wait()
        return carry

    lax.fori_loop(0, tm, wait_body, 0)
    g = gate_ref[...]
    o_ref[...] = x_ref[...] + (buf[slot, 0] * g[:, 0:1] + buf[slot, 1] * g[:, 1:2])


def moe_combine(dest_flat, x, gates, ys, tm):
    t, d = x.shape
    return pl.pallas_call(
        _combine_kernel,
        grid_spec=pltpu.PrefetchScalarGridSpec(
            num_scalar_prefetch=1,
            grid=(t // tm,),
            in_specs=[pl.BlockSpec((tm, d), lambda i, dr: (i, 0)),
                      pl.BlockSpec((tm, LANE), lambda i, dr: (i, 0)),
                      pl.BlockSpec(memory_space=pl.ANY)],
            out_specs=pl.BlockSpec((tm, d), lambda i, dr: (i, 0)),
            scratch_shapes=[pltpu.VMEM((2, 2, tm, d), F32), pltpu.SemaphoreType.DMA((2,))]),
        out_shape=jax.ShapeDtypeStruct((t, d), F32),
        compiler_params=_cparams(("arbitrary",), 48),
        name="moe_combine",
    )(dest_flat, x, gates, ys)


def moe_mixer(x, g, wr, wg, wu, wd, tm, blk, tf):
    t, d = x.shape
    idx, gates, counts = moe_router(x, g, wr, tm)
    experts = idx[:, 0:2]
    ranks = idx[:, 2:4]
    cnt = counts[0, :N_EXPERTS].astype(jnp.int32)
    padded = (cnt + blk - 1) // blk * blk
    pend = jnp.cumsum(padded)
    pstart = pend - padded
    dest = (pstart[experts] + ranks).reshape(-1).astype(jnp.int32)
    n_blocks = -(-(2 * t) // blk) + N_EXPERTS
    n_used = (pend[-1] // blk).astype(jnp.int32)
    blk_ids = jnp.minimum(jnp.arange(n_blocks, dtype=jnp.int32), n_used - 1)
    blk_e = jnp.minimum(jnp.searchsorted(pend, blk_ids * blk, side="right"), N_EXPERTS - 1).astype(jnp.int32)
    xs = moe_dispatch(dest, x, n_blocks * blk, tm)
    ys = expert_ffn(blk_e, n_used.reshape(1), xs, g, wg, wu, wd, blk, tf)
    return moe_combine(dest, x, gates, ys, tm)


def _rope_tables(pos):
    half = A_ROPE // 2
    inv = ROPE_THETA ** (-jnp.arange(half, dtype=F32) / half)
    ang = pos.astype(F32)[:, None] * inv[None, :]
    cos, sin = jnp.cos(ang), jnp.sin(ang)
    zeros = jnp.zeros((pos.shape[0], LANE - A_ROPE), F32)
    return (jnp.concatenate([cos, cos, zeros], axis=-1), jnp.concatenate([-sin, sin, zeros], axis=-1))


def _pad_lanes(v, width=LANE):
    return jnp.pad(v, (0, width - v.shape[0])).reshape(1, width)


def _layer_weights(l, g_mix_norm, w_in, b_if, g_mh, g_cq, w_uq, g_ckv, w_uk, w_uv, g_qn, g_qr, g_kn, g_kr,
                   w_out, g_ffn_norm, g_ple_norm, w_ple_gate, w_ple_proj):
    d = w_in.shape[1]
    wi = w_in[l]
    o = 0
    parts = {}
    for name, width in (("mq", M_HEADS * M_DQK), ("mk", M_HEADS * M_DQK), ("mv", M_WIDTH), ("mo", M_WIDTH),
                        ("mi", M_HEADS), ("mf", M_HEADS), ("cq", Q_RANK), ("ckv", KV_RANK), ("kpe", A_ROPE)):
        parts[name] = wi[:, o:o + width]
        o += width
    used = COL_KG + A_ROPE + 2 * M_HEADS
    w_in_p = jnp.concatenate([parts[k] for k in ("mq", "mk", "mv", "mo", "cq", "ckv", "kpe", "mi", "mf")]
                             + [jnp.zeros((d, N_IN_PAD - used), F32)], axis=1).astype(BF16)
    wuq = w_uq[l].reshape(Q_RANK, A_HEADS, A_NOPE + A_ROPE)
    wuq = jnp.pad(wuq, ((0, 0), (0, 0), (0, QK_HEAD - A_NOPE - A_ROPE))).reshape(Q_RANK, A_HEADS * QK_HEAD)
    bias = jnp.zeros((LANE,), F32).at[GATE_I:GATE_I + 2 * M_HEADS].set(b_if[l].astype(F32)).reshape(1, LANE)
    wuk = w_uk[l].astype(BF16)
    return {
        "g_mix": g_mix_norm[l].reshape(1, d), "w_in": w_in_p, "bias": bias,
        "g_mh": g_mh[l].reshape(1, M_WIDTH), "g_cq": g_cq[l].reshape(1, Q_RANK),
        "w_uq": wuq.astype(BF16), "g_ckv": g_ckv[l].reshape(1, KV_RANK),
        "w_uk": wuk, "w_ukt": wuk.T, "w_uv": w_uv[l].astype(BF16),
        "g_qn": g_qn[l].reshape(1, A_NOPE), "g_qr": _pad_lanes(g_qr[l]),
        "g_kn": g_kn[l].reshape(1, A_NOPE), "g_kr": _pad_lanes(g_kr[l]),
        "w_out": w_out[l].astype(BF16), "g_ffn": g_ffn_norm[l].reshape(1, d),
        "g_ple": g_ple_norm[l].reshape(1, d), "w_ple_gate": w_ple_gate[l].astype(BF16),
        "w_ple_proj": w_ple_proj[l].astype(BF16),
    }


def _row_tile(t, pref):
    return pref if t % pref == 0 else t


def kernel(x_prompt, x_sample, p_prompt, p_sample, cache_ckv, cache_kpe, state_C, state_n, state_m, page_table, g_mix_norm, w_in, b_if, g_mh, g_cq, w_uq, g_ckv, w_uk, w_uv, g_qn, g_qr, g_kn, g_kr, w_out, g_ffn_norm, w_gate, w_up, w_down, w_router, w_gate_e, w_up_e, w_down_e, g_ple_norm, w_ple_gate, w_ple_proj):
    depth = w_in.shape[0]
    bp, sp, d = x_prompt.shape
    bs, ss, _ = x_sample.shape
    n_pages = page_table.shape[1]
    past = n_pages * PAGE_SIZE
    tp = bp * sp
    ts = bs * ss
    tm_p = _row_tile(sp, 512)
    tm_s = ts
    chunk = next((c for c in (MLSTM_CHUNK, M_CHUNK) if sp % c == 0), sp)
    pad_s = 16

    cos_p, sin_p = _rope_tables(jnp.arange(sp))
    cos_s, sin_s = _rope_tables(past + jnp.arange(ss))
    cos_s = jnp.broadcast_to(cos_s[None], (bs, ss, LANE)).reshape(ts, LANE)
    sin_s = jnp.broadcast_to(sin_s[None], (bs, ss, LANE)).reshape(ts, LANE)

    cache_kpe_t = jnp.swapaxes(cache_kpe, 2, 3)
    y_p = x_prompt.reshape(tp, d)
    y_s = x_sample.reshape(ts, d)
    outs = {k: [] for k in ("ckv_p", "kpe_p", "C_p", "n_p", "m_p", "ckv_s", "kpe_s", "C_s", "n_s", "m_s")}
    zc = jnp.zeros((1, bp, M_HEADS, M_DQK, M_DV), F32)
    group_s = next(g for g in (4, 2, 1) if ts % g == 0)
    zn = jnp.zeros((bp, M_HEADS, 1, M_DQK), F32)
    zm = jnp.zeros((bp, M_HEADS, 1, LANE), F32)

    for l in range(depth):
        lw = _layer_weights(l, g_mix_norm, w_in, b_if, g_mh, g_cq, w_uq, g_ckv, w_uk, w_uv, g_qn, g_qr, g_kn,
                            g_kr, w_out, g_ffn_norm, g_ple_norm, w_ple_gate, w_ple_proj)

        z_p = norm_matmul(y_p, lw["g_mix"], lw["w_in"], tm_p, 2048)
        ckv_p, kpe_p, q_p, k_p, v_p = mla_prep(z_p, cos_p, sin_p, lw, tm_p, sp // tm_p, False)
        hm_p, c1, n1, m1 = mlstm(z_p, lw["bias"], lw["g_mh"], zc, zn, zm, 0, bp, sp // chunk, chunk, chunk, 1)
        ha_p = prompt_attention(q_p, k_p, v_p, bp, sp, tm_p)
        y_p = out_proj(hm_p, ha_p, lw["w_out"], y_p, tm_p, 2048)
        outs["ckv_p"].append(ckv_p.reshape(bp, sp, KV_RANK))
        outs["kpe_p"].append(kpe_p.reshape(bp, sp, A_ROPE))
        outs["C_p"].append(c1)
        outs["n_p"].append(n1.reshape(bp, M_HEADS, M_DQK))
        outs["m_p"].append(m1[:, :, 0, 0])

        z_s = norm_matmul(y_s, lw["g_mix"], lw["w_in"], tm_s, 1024)
        ckv_s, kpe_s, q_s, k_s, _, a_s = mla_prep(z_s, cos_s, sin_s, lw, tm_s, 1, True)
        z_s8 = jnp.pad(z_s.reshape(ts, 1, N_IN_PAD), ((0, 0), (pad_s - 1, 0), (0, 0))).reshape(ts * pad_s, N_IN_PAD)
        hm_s8, c2, n2, m2 = mlstm(z_s8, lw["bias"], lw["g_mh"], state_C,
                                  state_n[l].reshape(bs, M_HEADS, 1, M_DQK),
                                  jnp.broadcast_to(state_m[l][:, :, None, None], (bs, M_HEADS, 1, LANE)),
                                  l, ts, 1, pad_s, 1, group_s)
        outs["C_s"].append(c2)
        hm_s = hm_s8.reshape(ts, pad_s, M_WIDTH)[:, pad_s - 1, :]
        ha_s = decode_attention(l, page_table, cache_ckv, cache_kpe_t,
                                q_s.reshape(ts, A_HEADS, QK_HEAD), k_s.reshape(ts, A_HEADS, QK_HEAD),
                                a_s.reshape(ts, A_HEADS, KV_RANK), ckv_s.reshape(ts, 1, KV_RANK),
                                lw["w_ukt"], lw["w_uv"], next(p for p in (16, 8, 1) if n_pages % p == 0))
        y_s = out_proj(hm_s, ha_s.reshape(ts, A_WIDTH).astype(BF16), lw["w_out"], y_s, tm_s, 1024)
        outs["ckv_s"].append(ckv_s.reshape(bs, ss, KV_RANK))
        outs["kpe_s"].append(kpe_s.reshape(bs, ss, A_ROPE))
        outs["n_s"].append(n2.reshape(bs, M_HEADS, M_DQK))
        outs["m_s"].append(m2[:, :, 0, 0])

        if l % 2 == 0:
            wg, wu, wd = (w[l // 2].astype(BF16) for w in (w_gate, w_up, w_down))
            tf = 512 if wg.shape[1] % 512 == 0 else wg.shape[1]
            y_p = ffn_dense(y_p, lw["g_ffn"], wg, wu, wd, tm_p, tf)
            y_s = ffn_dense(y_s, lw["g_ffn"], wg, wu, wd, tm_s, tf)
        else:
            wg, wu, wd = (w[l // 2].astype(BF16) for w in (w_gate_e, w_up_e, w_down_e))
            wr = jnp.pad(w_router[l // 2], ((0, 0), (0, LANE - N_EXPERTS))).astype(BF16)
            tf = 512 if wg.shape[2] % 512 == 0 else wg.shape[2]
            y_p = moe_mixer(y_p, lw["g_ffn"], wr, wg, wu, wd, min(tm_p, 256), tm_p, tf)
            y_s = moe_mixer(y_s, lw["g_ffn"], wr, wg, wu, wd, tm_s, 128, tf)
        y_p = ple(y_p, lw["g_ple"], p_prompt[l].reshape(tp, -1), lw["w_ple_gate"], lw["w_ple_proj"], tm_p, 2048)
        y_s = ple(y_s, lw["g_ple"], p_sample[l].reshape(ts, -1), lw["w_ple_gate"], lw["w_ple_proj"], tm_s, 1024)

    st = lambda k: jnp.stack(outs[k])
    return (y_p.reshape(bp, sp, d), y_s.reshape(bs, ss, d),
            st("ckv_p"), st("kpe_p"), st("C_p"), st("n_p"), st("m_p"),
            st("ckv_s"), st("kpe_s"), st("C_s"), st("n_s"), st("m_s"))
```

```python
import functools

import jax
import jax.numpy as jnp
from jax import lax
from jax.experimental import pallas as pl
from jax.experimental.pallas import tpu as pltpu

F32 = jnp.float32
BF16 = jnp.bfloat16

M_HEADS = 4
M_DQK = 128
M_DV = 256
M_WIDTH = M_HEADS * M_DV
M_CHUNK = 64
MLSTM_CHUNK = 256
A_HEADS = 8
A_NOPE = 128
A_ROPE = 64
A_VD = 128
A_WIDTH = A_HEADS * A_VD
Q_RANK = 512
KV_RANK = 256
ROPE_THETA = 10000.0
MLA_SCALE = (A_NOPE + A_ROPE) ** -0.5
LOG2_E = 1.4426950408889634
PAGE_SIZE = 128
N_EXPERTS = 8
EPS = 1e-6

LANE = 128
MIB = 2 ** 20

COL_MQ = 0
COL_MK = COL_MQ + M_HEADS * M_DQK
COL_MV = COL_MK + M_HEADS * M_DQK
COL_MO = COL_MV + M_WIDTH
COL_CQ = COL_MO + M_WIDTH
COL_CKV = COL_CQ + Q_RANK
COL_KG = COL_CKV + KV_RANK
N_IN_PAD = 4096
GATE_I = A_ROPE
GATE_F = A_ROPE + M_HEADS
QK_HEAD = 2 * LANE
NEG_BIG = -1e30


def _cparams(sem, vmem_mib):
    return pltpu.CompilerParams(dimension_semantics=sem, vmem_limit_bytes=vmem_mib * MIB)


def _rms(x, g, n):
    ms = jnp.sum(x * x, axis=-1, keepdims=True) * (1.0 / n)
    return x * lax.rsqrt(ms + EPS) * g


def _dot(a, b):
    return jnp.dot(a, b, preferred_element_type=F32)


def _dot_nt(a, b):
    return lax.dot_general(a, b, (((1,), (1,)), ((), ())), preferred_element_type=F32)


def _dot_tn(a, b):
    return lax.dot_general(a, b, (((0,), (0,)), ((), ())), preferred_element_type=F32)


def _norm_matmul_kernel(x_ref, g_ref, w_ref, o_ref, xn_ref):
    @pl.when(pl.program_id(1) == 0)
    def _():
        xn_ref[...] = _rms(x_ref[...], g_ref[...], x_ref.shape[-1]).astype(BF16)

    o_ref[...] = _dot(xn_ref[...], w_ref[...])


def norm_matmul(x, g, w, tm, tn):
    t, d = x.shape
    n = w.shape[1]
    return pl.pallas_call(
        _norm_matmul_kernel,
        grid=(t // tm, n // tn),
        in_specs=[pl.BlockSpec((tm, d), lambda i, j: (i, 0)),
                  pl.BlockSpec((1, d), lambda i, j: (0, 0)),
                  pl.BlockSpec((d, tn), lambda i, j: (0, j))],
        out_specs=pl.BlockSpec((tm, tn), lambda i, j: (i, j)),
        out_shape=jax.ShapeDtypeStruct((t, n), F32),
        scratch_shapes=[pltpu.VMEM((tm, d), BF16)],
        compiler_params=_cparams(("parallel", "arbitrary"), 40),
        name="norm_matmul",
    )(x, g, w)


def _mla_prep_kernel(cq_ref, ckv_ref, kg_ref, cos_ref, sin_ref, wuq_ref, wuk_ref, wuv_ref,
                     gcq_ref, gckv_ref, gkr_ref, gqn_ref, gqr_ref, gkn_ref,
                     ckv_out, kpe_out, q_out, k_out, v_out, *a_out, absorb):
    cos = cos_ref[...]
    sin = sin_ref[...]

    def rope(x):
        return x * cos + (pltpu.roll(x, A_ROPE // 2, 1) + pltpu.roll(x, LANE - A_ROPE // 2, 1)) * sin

    ckv = _rms(ckv_ref[...], gckv_ref[...], KV_RANK)
    ckv_out[...] = ckv
    ckv_b = ckv.astype(BF16)

    kg = kg_ref[...]
    lane = lax.broadcasted_iota(jnp.int32, kg.shape, 1)
    kp = jnp.where(lane < A_ROPE, kg, 0.0)
    kpr = rope(_rms(kp, gkr_ref[...], A_ROPE))
    kpe_out[...] = kpr[:, :A_ROPE]
    kpr_b = kpr.astype(BF16)

    kn = _dot(ckv_b, wuk_ref[...])
    v_out[...] = _dot(ckv_b, wuv_ref[...]).astype(BF16)
    cq = _rms(cq_ref[...], gcq_ref[...], Q_RANK).astype(BF16)
    qf = _dot(cq, wuq_ref[...])
    gkn = gkn_ref[...]
    for h in range(A_HEADS):
        n0 = h * A_NOPE
        c0 = h * QK_HEAD
        k_out[:, c0:c0 + LANE] = _rms(kn[:, n0:n0 + A_NOPE], gkn, A_NOPE).astype(BF16)
        k_out[:, c0 + LANE:c0 + QK_HEAD] = kpr_b
        qn = _rms(qf[:, c0:c0 + LANE], gqn_ref[...], A_NOPE)
        q_out[:, c0:c0 + LANE] = qn.astype(BF16)
        qp = _rms(qf[:, c0 + LANE:c0 + QK_HEAD], gqr_ref[...], A_ROPE)
        q_out[:, c0 + LANE:c0 + QK_HEAD] = rope(qp).astype(BF16)
        if absorb:
            a_out[0][:, h * KV_RANK:(h + 1) * KV_RANK] = _dot_nt(
                (qn * gkn).astype(BF16), wuk_ref[:, n0:n0 + A_NOPE])


def mla_prep(z, cos, sin, lw, tm, n_pos_blocks, absorb):
    t = z.shape[0]
    row = lambda i: (i, 0)
    full = lambda i: (0, 0)
    out_shape = [jax.ShapeDtypeStruct((t, KV_RANK), F32),
                 jax.ShapeDtypeStruct((t, A_ROPE), F32),
                 jax.ShapeDtypeStruct((t, A_HEADS * QK_HEAD), BF16),
                 jax.ShapeDtypeStruct((t, A_HEADS * QK_HEAD), BF16),
                 jax.ShapeDtypeStruct((t, A_WIDTH), BF16)]
    out_specs = [pl.BlockSpec((tm, KV_RANK), row), pl.BlockSpec((tm, A_ROPE), row),
                 pl.BlockSpec((tm, A_HEADS * QK_HEAD), row), pl.BlockSpec((tm, A_HEADS * QK_HEAD), row),
                 pl.BlockSpec((tm, A_WIDTH), row)]
    if absorb:
        out_shape.append(jax.ShapeDtypeStruct((t, A_HEADS * KV_RANK), F32))
        out_specs.append(pl.BlockSpec((tm, A_HEADS * KV_RANK), row))
    return pl.pallas_call(
        functools.partial(_mla_prep_kernel, absorb=absorb),
        grid=(t // tm,),
        in_specs=[pl.BlockSpec((tm, Q_RANK), lambda i: (i, COL_CQ // Q_RANK)),
                  pl.BlockSpec((tm, KV_RANK), lambda i: (i, COL_CKV // KV_RANK)),
                  pl.BlockSpec((tm, LANE), lambda i: (i, COL_KG // LANE)),
                  pl.BlockSpec((tm, LANE), lambda i: (i % n_pos_blocks, 0)),
                  pl.BlockSpec((tm, LANE), lambda i: (i % n_pos_blocks, 0)),
                  pl.BlockSpec((Q_RANK, A_HEADS * QK_HEAD), full),
                  pl.BlockSpec((KV_RANK, A_HEADS * A_NOPE), full),
                  pl.BlockSpec((KV_RANK, A_WIDTH), full),
                  pl.BlockSpec((1, Q_RANK), full), pl.BlockSpec((1, KV_RANK), full),
                  pl.BlockSpec((1, LANE), full), pl.BlockSpec((1, LANE), full),
                  pl.BlockSpec((1, LANE), full), pl.BlockSpec((1, LANE), full)],
        out_specs=out_specs,
        out_shape=out_shape,
        compiler_params=_cparams(("parallel",), 48),
        name="mla_prep",
    )(z, z, z, cos, sin, lw["w_uq"], lw["w_uk"], lw["w_uv"], lw["g_cq"], lw["g_ckv"],
      lw["g_kr"], lw["g_qn"], lw["g_qr"], lw["g_kn"])


def _mlstm_kernel(q_ref, k_ref, v_ref, o_ref, kg_ref, bias_ref, gmh_ref, c0_ref, n0_ref, m0_ref,
                  h_out, c_out, n_out, m_out, c_s, n_s, m_s, *, chunk, n_real, group):
    c = pl.program_id(1)

    @pl.when(c == 0)
    def _():
        c_s[...] = c0_ref[...]
        n_s[...] = n0_ref[...]
        m_s[...] = m0_ref[...]

    L = chunk
    gates_all = kg_ref[...] + bias_ref[...]
    lf_all_rows = jax.nn.log_sigmoid(gates_all)
    ri = lax.broadcasted_iota(jnp.int32, (L, L), 0)
    ci = lax.broadcasted_iota(jnp.int32, (L, L), 1)
    tri = ci <= ri
    eye = ci == ri
    rowid = lax.broadcasted_iota(jnp.int32, (L, 1), 0)
    real = rowid >= (L - n_real)
    scale = M_DQK ** -0.5

    def to_row(col):
        return jnp.sum(jnp.where(eye, col, 0.0), axis=0, keepdims=True)

    for sh in range(group * M_HEADS):
        s, h = divmod(sh, M_HEADS)
        r0 = s * L
        i_col = gates_all[r0:r0 + L, GATE_I + h:GATE_I + h + 1]
        lf_col = lf_all_rows[r0:r0 + L, GATE_F + h:GATE_F + h + 1]
        if n_real < L:
            i_col = jnp.where(real, i_col, NEG_BIG)
            lf_col = jnp.where(real, lf_col, 0.0)
        i_row = to_row(i_col)
        lf_row = to_row(lf_col)
        bh_col = jnp.sum(jnp.where(tri, lf_row, 0.0), axis=1, keepdims=True)
        bh_row = to_row(bh_col)
        dmat = jnp.where(tri, bh_col - bh_row + i_row, -jnp.inf)
        m_prev = m_s[s, h][:, 0:1]
        inter = bh_col + m_prev
        m_t = jnp.maximum(inter, jnp.max(dmat, axis=1, keepdims=True))
        w_inter = jnp.exp(inter - m_t)
        qh = q_ref[r0:r0 + L, h * M_DQK:(h + 1) * M_DQK]
        kh = k_ref[r0:r0 + L, h * M_DQK:(h + 1) * M_DQK]
        vh = v_ref[r0:r0 + L, h * M_DV:(h + 1) * M_DV]
        qb = qh.astype(BF16)
        vb = vh.astype(BF16)
        s_qk = _dot_nt(qb, kh.astype(BF16)) * scale * jnp.exp(dmat - m_t)
        c_prev = c_s[s, h]
        n_prev = n_s[s, h]
        num = _dot(s_qk.astype(BF16), vb) + _dot(qb, c_prev.astype(BF16)) * (scale * w_inter)
        qn = jnp.sum(qh * n_prev, axis=1, keepdims=True) * scale
        den = jnp.sum(s_qk, axis=1, keepdims=True) + w_inter * qn
        den = jnp.maximum(jnp.abs(den), jnp.exp(-m_t))
        hh = num / den
        m_last = m_t[L - 1:L, :]
        bh_last = bh_col[L - 1:L, :]
        g_inter = jnp.exp(bh_last + m_prev - m_last)
        g_intra = jnp.exp(bh_last - bh_col + i_col - m_last)
        kw = kh * g_intra
        c_s[s, h] = g_inter * c_prev + _dot_tn(kw.astype(BF16), vb)
        n_s[s, h] = g_inter * n_prev + jnp.sum(kw, axis=0, keepdims=True)
        m_s[s, h] = jnp.broadcast_to(m_last, (1, LANE))
        hn = _rms(hh, gmh_ref[:, h * M_DV:(h + 1) * M_DV], M_DV)
        og = o_ref[r0:r0 + L, h * M_DV:(h + 1) * M_DV]
        h_out[r0:r0 + L, h * M_DV:(h + 1) * M_DV] = (hn * jax.nn.sigmoid(og)).astype(BF16)

    @pl.when(c == pl.num_programs(1) - 1)
    def _():
        c_out[...] = c_s[...]
        n_out[...] = n_s[...]
        m_out[...] = m_s[...]


def mlstm(z, bias, gmh, c0_all, n0, m0, layer, batch, n_chunks, chunk, n_real, group):
    assert group == 1 or n_chunks == 1
    t = z.shape[0]
    rows = group * chunk
    blk = lambda w, col: pl.BlockSpec((rows, w), lambda b, c: (b * n_chunks + c, col))
    st = lambda shape: pl.BlockSpec((group,) + shape[1:], lambda b, c: (b, 0, 0, 0))
    c_shape = c0_all.shape[1:]
    full = lambda b, c: (0, 0)
    return pl.pallas_call(
        functools.partial(_mlstm_kernel, chunk=chunk, n_real=n_real, group=group),
        grid=(batch // group, n_chunks),
        in_specs=[blk(M_HEADS * M_DQK, COL_MQ // (M_HEADS * M_DQK)),
                  blk(M_HEADS * M_DQK, COL_MK // (M_HEADS * M_DQK)),
                  blk(M_WIDTH, COL_MV // M_WIDTH), blk(M_WIDTH, COL_MO // M_WIDTH),
                  blk(LANE, COL_KG // LANE),
                  pl.BlockSpec((1, LANE), full), pl.BlockSpec((1, M_WIDTH), full),
                  pl.BlockSpec((None, group) + c_shape[1:], lambda b, c: (layer, b, 0, 0, 0)),
                  st(n0.shape), st(m0.shape)],
        out_specs=[pl.BlockSpec((rows, M_WIDTH), lambda b, c: (b * n_chunks + c, 0)),
                   st(c_shape), st(n0.shape), st(m0.shape)],
        out_shape=[jax.ShapeDtypeStruct((t, M_WIDTH), BF16),
                   jax.ShapeDtypeStruct(c_shape, F32), jax.ShapeDtypeStruct(n0.shape, F32),
                   jax.ShapeDtypeStruct(m0.shape, F32)],
        scratch_shapes=[pltpu.VMEM((group,) + c_shape[1:], F32), pltpu.VMEM((group,) + n0.shape[1:], F32),
                        pltpu.VMEM((group,) + m0.shape[1:], F32)],
        compiler_params=_cparams(("parallel", "arbitrary"), 32),
        name="mlstm",
    )(z, z, z, z, z, bias, gmh, c0_all, n0, m0)


def _attn_kernel(q_ref, k_ref, v_ref, o_ref, *, tq, nq):
    qi = pl.program_id(2)
    q = q_ref[...]
    c2 = MLA_SCALE * LOG2_E
    row = lax.broadcasted_iota(jnp.int32, (tq, tq), 0)
    col = lax.broadcasted_iota(jnp.int32, (tq, tq), 1)

    for n in range(nq):
        @pl.when(qi == n)
        def _(n=n):
            lo = n * tq
            for hh in range(ATTN_HEADS_PER_STEP):
                qc = slice(hh * QK_HEAD, (hh + 1) * QK_HEAD)
                vc = slice(hh * A_VD, (hh + 1) * A_VD)
                qh = q[:, qc]
                s_d = jnp.where(col <= row, _dot_nt(qh, k_ref[lo:lo + tq, qc]) * c2, -jnp.inf)
                m = jnp.max(s_d, axis=1, keepdims=True)
                if n:
                    s_o = _dot_nt(qh, k_ref[0:lo, qc]) * c2
                    m = jnp.maximum(m, jnp.max(s_o, axis=1, keepdims=True))
                p_d = jnp.exp2(s_d - m)
                l = jnp.sum(p_d, axis=1, keepdims=True)
                acc = _dot(p_d.astype(BF16), v_ref[lo:lo + tq, vc])
                if n:
                    p_o = jnp.exp2(s_o - m)
                    l = l + jnp.sum(p_o, axis=1, keepdims=True)
                    acc = acc + _dot(p_o.astype(BF16), v_ref[0:lo, vc])
                o_ref[:, vc] = (acc / l).astype(o_ref.dtype)


ATTN_HEADS_PER_STEP = 4


def prompt_attention(q, k, v, batch, seq, tq):
    nq = seq // tq
    hps = ATTN_HEADS_PER_STEP
    return pl.pallas_call(
        functools.partial(_attn_kernel, tq=tq, nq=nq),
        grid=(batch, A_HEADS // hps, nq),
        in_specs=[pl.BlockSpec((tq, hps * QK_HEAD), lambda b, h, i: (b * nq + i, h)),
                  pl.BlockSpec((seq, hps * QK_HEAD), lambda b, h, i: (b, h)),
                  pl.BlockSpec((seq, hps * A_VD), lambda b, h, i: (b, h))],
        out_specs=pl.BlockSpec((tq, hps * A_VD), lambda b, h, i: (b * nq + i, h)),
        out_shape=jax.ShapeDtypeStruct((batch * seq, A_WIDTH), BF16),
        compiler_params=_cparams(("parallel", "parallel", "arbitrary"), 32),
        name="prompt_attention",
    )(q, k, v)


DECODE_SLOTS = 3
DECODE_SEQS = 2


def _decode_attn_kernel(pt_ref, q_ref, knew_ref, a_ref, cnew_ref, wukt_ref, wuv_ref, ckv_hbm, kpe_hbm, o_ref,
                        lhs_s, cbuf, kbuf, cb_s, sem_c, sem_k, *, layer, pps, n_chunks):
    step = pl.program_id(0)
    n_steps = pl.num_programs(0)
    total = n_steps * n_chunks
    n_up = A_HEADS * A_NOPE
    tc = pps * PAGE_SIZE

    def start_chunk(g):
        slot = g % DECODE_SLOTS
        gg = jnp.minimum(g, total - 1)
        gs = gg // n_chunks
        gc = gg % n_chunks
        for s in range(DECODE_SEQS):
            for p in range(pps):
                page = pt_ref[gs * DECODE_SEQS + s, gc * pps + p]
                pltpu.make_async_copy(ckv_hbm.at[layer, page], cbuf.at[slot, s, p], sem_c.at[slot, s]).start()
                pltpu.make_async_copy(kpe_hbm.at[layer, page], kbuf.at[slot, s, p], sem_k.at[slot, s]).start()

    def wait_chunk(slot, s):
        pltpu.make_async_copy(ckv_hbm.at[layer, pl.ds(0, pps)], cbuf.at[slot, s], sem_c.at[slot, s]).wait()
        pltpu.make_async_copy(kpe_hbm.at[layer, pl.ds(0, pps)], kbuf.at[slot, s], sem_k.at[slot, s]).wait()

    @pl.when(step == 0)
    def _():
        lhs_s[0:n_up, :] = wukt_ref[...]
        start_chunk(0)
        start_chunk(1)

    lhs_s[n_up:, :] = jnp.concatenate([a_ref[s] for s in range(DECODE_SEQS)], axis=0).astype(BF16)
    hid = lax.broadcasted_iota(jnp.int32, (A_HEADS, tc), 0)
    qpe = [q_ref[s][:, LANE:LANE + A_ROPE] for s in range(DECODE_SEQS)]

    def scores(c, s):
        slot = (step * n_chunks + c) % DECODE_SLOTS
        cb = cbuf[slot, s].reshape(tc, KV_RANK).astype(BF16)
        cb_s[c % 2, s] = cb
        up = _dot_nt(lhs_s[...], cb)
        ss = jnp.zeros((A_HEADS, tc), F32)
        for h in range(A_HEADS):
            blk = up[h * A_NOPE:(h + 1) * A_NOPE, :]
            ss = jnp.where(hid == h, jnp.sum(blk * blk, axis=0, keepdims=True), ss)
        r0 = n_up + s * A_HEADS
        s_nope = up[r0:r0 + A_HEADS, :] * lax.rsqrt(ss * (1.0 / A_NOPE) + EPS)
        s_pe = jnp.concatenate([_dot(qpe[s], kbuf[slot, s, p].astype(BF16)) for p in range(pps)], axis=1)
        return (s_nope + s_pe) * MLA_SCALE

    def accumulate(c, s, sc, state):
        m_old, l_old, acc_old = state
        m_new = jnp.maximum(m_old, jnp.max(sc, axis=1, keepdims=True))
        alpha = jnp.exp(m_old - m_new)
        p_att = jnp.exp(sc - m_new)
        l_new = alpha * l_old + jnp.sum(p_att, axis=1, keepdims=True)
        acc_new = alpha * acc_old + _dot(p_att.astype(BF16), cb_s[c % 2, s])
        return m_new, l_new, acc_new

    def all_scores(c):
        g = step * n_chunks + c
        for s in range(DECODE_SEQS):
            wait_chunk(g % DECODE_SLOTS, s)
        out = [scores(c, 0)]
        start_chunk(g + 2)
        out += [scores(c, s) for s in range(1, DECODE_SEQS)]
        return tuple(out)

    def body(c, carry):
        cur = all_scores(c)
        new = []
        for s in range(DECODE_SEQS):
            prev_sc, st = carry[s][0], carry[s][1:]
            new.append((cur[s],) + accumulate(c - 1, s, prev_sc, st))
        return tuple(new)

    first = all_scores(0)
    init = []
    for s in range(DECODE_SEQS):
        q = q_ref[s]
        s_new = jnp.sum(q.astype(F32) * knew_ref[s].astype(F32), axis=1, keepdims=True) * MLA_SCALE
        init.append((first[s], s_new, jnp.ones((A_HEADS, 1), F32),
                     jnp.broadcast_to(cnew_ref[s], (A_HEADS, KV_RANK))))
    carry = lax.fori_loop(1, n_chunks, body, tuple(init))

    hid_o = lax.broadcasted_iota(jnp.int32, (A_HEADS, A_VD), 0)
    for s in range(DECODE_SEQS):
        _, l_fin, acc_fin = accumulate(n_chunks - 1, s, carry[s][0], carry[s][1:])
        o_lat = acc_fin / l_fin
        full = _dot(o_lat.astype(BF16), wuv_ref[...])
        res = jnp.zeros((A_HEADS, A_VD), F32)
        for h in range(A_HEADS):
            res = jnp.where(hid_o == h, full[:, h * A_VD:(h + 1) * A_VD], res)
        o_ref[s] = res

    @pl.when(step == n_steps - 1)
    def _():
        for s in range(DECODE_SEQS):
            wait_chunk(total % DECODE_SLOTS, s)
            wait_chunk((total + 1) % DECODE_SLOTS, s)


def decode_attention(layer, page_table, cache_ckv, cache_kpe_t, q3, k3, a3, cnew3, wukt, wuv, pps):
    nb, n_pages = page_table.shape
    assert nb % DECODE_SEQS == 0
    n_up = A_HEADS * A_NOPE
    ns = DECODE_SEQS
    per_seq = lambda rows, w: pl.BlockSpec((ns, rows, w), lambda b, pt: (b, 0, 0))
    full = lambda b, pt: (0, 0)
    return pl.pallas_call(
        functools.partial(_decode_attn_kernel, layer=layer, pps=pps, n_chunks=n_pages // pps),
        grid_spec=pltpu.PrefetchScalarGridSpec(
            num_scalar_prefetch=1,
            grid=(nb // ns,),
            in_specs=[per_seq(A_HEADS, QK_HEAD), per_seq(A_HEADS, QK_HEAD), per_seq(A_HEADS, KV_RANK),
                      per_seq(1, KV_RANK),
                      pl.BlockSpec((n_up, KV_RANK), full), pl.BlockSpec((KV_RANK, A_WIDTH), full),
                      pl.BlockSpec(memory_space=pl.ANY), pl.BlockSpec(memory_space=pl.ANY)],
            out_specs=pl.BlockSpec((ns, A_HEADS, A_VD), lambda b, pt: (b, 0, 0)),
            scratch_shapes=[pltpu.VMEM((n_up + ns * A_HEADS, KV_RANK), BF16),
                            pltpu.VMEM((DECODE_SLOTS, ns, pps, PAGE_SIZE, KV_RANK), F32),
                            pltpu.VMEM((DECODE_SLOTS, ns, pps, A_ROPE, PAGE_SIZE), F32),
                            pltpu.VMEM((2, ns, pps * PAGE_SIZE, KV_RANK), BF16),
                            pltpu.SemaphoreType.DMA((DECODE_SLOTS, ns)),
                            pltpu.SemaphoreType.DMA((DECODE_SLOTS, ns))]),
        out_shape=jax.ShapeDtypeStruct((nb, A_HEADS, A_VD), F32),
        compiler_params=_cparams(("arbitrary",), 56),
        name="decode_attention",
    )(page_table, q3, k3, a3, cnew3, wukt, wuv, cache_ckv, cache_kpe_t)


def _out_proj_kernel(hm_ref, ha_ref, w_ref, x_ref, o_ref):
    acc = _dot(hm_ref[...], w_ref[0:M_WIDTH, :]) + _dot(ha_ref[...], w_ref[M_WIDTH:, :])
    o_ref[...] = x_ref[...] + acc


def out_proj(hm, ha, w, x, tm, tn):
    t, d = x.shape
    return pl.pallas_call(
        _out_proj_kernel,
        grid=(t // tm, d // tn),
        in_specs=[pl.BlockSpec((tm, M_WIDTH), lambda i, j: (i, 0)),
                  pl.BlockSpec((tm, A_WIDTH), lambda i, j: (i, 0)),
                  pl.BlockSpec((M_WIDTH + A_WIDTH, tn), lambda i, j: (0, j)),
                  pl.BlockSpec((tm, tn), lambda i, j: (i, j))],
        out_specs=pl.BlockSpec((tm, tn), lambda i, j: (i, j)),
        out_shape=jax.ShapeDtypeStruct((t, d), F32),
        compiler_params=_cparams(("parallel", "arbitrary"), 40),
        name="out_proj",
    )(hm, ha, w, x)


def _ffn_kernel(x_ref, g_ref, wg_ref, wu_ref, wd_ref, o_ref, xn_ref, acc_ref):
    f = pl.program_id(1)

    @pl.when(f == 0)
    def _():
        xn_ref[...] = _rms(x_ref[...], g_ref[...], x_ref.shape[-1]).astype(BF16)
        acc_ref[...] = jnp.zeros(acc_ref.shape, F32)

    xn = xn_ref[...]
    a = jax.nn.silu(_dot(xn, wg_ref[...])) * _dot(xn, wu_ref[...])
    acc_ref[...] += _dot(a.astype(BF16), wd_ref[...])

    @pl.when(f == pl.num_programs(1) - 1)
    def _():
        o_ref[...] = x_ref[...] + acc_ref[...]


def ffn_dense(x, g, wg, wu, wd, tm, tf):
    t, d = x.shape
    ff = wg.shape[1]
    return pl.pallas_call(
        _ffn_kernel,
        grid=(t // tm, ff // tf),
        in_specs=[pl.BlockSpec((tm, d), lambda i, f: (i, 0)),
                  pl.BlockSpec((1, d), lambda i, f: (0, 0)),
                  pl.BlockSpec((d, tf), lambda i, f: (0, f)),
                  pl.BlockSpec((d, tf), lambda i, f: (0, f)),
                  pl.BlockSpec((tf, d), lambda i, f: (f, 0))],
        out_specs=pl.BlockSpec((tm, d), lambda i, f: (i, 0)),
        out_shape=jax.ShapeDtypeStruct((t, d), F32),
        scratch_shapes=[pltpu.VMEM((tm, d), BF16), pltpu.VMEM((tm, d), F32)],
        compiler_params=_cparams(("parallel", "arbitrary"), 48),
        name="ffn_dense",
    )(x, g, wg, wu, wd)


def _ple_kernel(x_ref, g_ref, p_ref, wg_ref, wp_ref, o_ref, xn_ref, *, tn):
    j = pl.program_id(1)

    @pl.when(j == 0)
    def _():
        xn_ref[...] = _rms(x_ref[...], g_ref[...], x_ref.shape[-1]).astype(BF16)

    gate = jax.nn.sigmoid(_dot(xn_ref[...], wg_ref[...]))
    proj = _dot(p_ref[...].astype(BF16), wp_ref[...])
    o_ref[...] = x_ref[:, pl.ds(pl.multiple_of(j * tn, tn), tn)] + gate * proj


def ple(x, g, p, wg, wp, tm, tn):
    t, d = x.shape
    pd = p.shape[1]
    return pl.pallas_call(
        functools.partial(_ple_kernel, tn=tn),
        grid=(t // tm, d // tn),
        in_specs=[pl.BlockSpec((tm, d), lambda i, j: (i, 0)),
                  pl.BlockSpec((1, d), lambda i, j: (0, 0)),
                  pl.BlockSpec((tm, pd), lambda i, j: (i, 0)),
                  pl.BlockSpec((d, tn), lambda i, j: (0, j)),
                  pl.BlockSpec((pd, tn), lambda i, j: (0, j))],
        out_specs=pl.BlockSpec((tm, tn), lambda i, j: (i, j)),
        out_shape=jax.ShapeDtypeStruct((t, d), F32),
        scratch_shapes=[pltpu.VMEM((tm, d), BF16)],
        compiler_params=_cparams(("parallel", "arbitrary"), 40),
        name="ple",
    )(x, g, p, wg, wp)


def _router_kernel(x_ref, g_ref, wr_ref, cnt_in, idx_out, gate_out, cnt_out, carry_s):
    i = pl.program_id(0)

    @pl.when(i == 0)
    def _():
        carry_s[...] = cnt_in[...]

    hn = _rms(x_ref[...], g_ref[...], x_ref.shape[-1])
    logits = _dot(hn.astype(BF16), wr_ref[...])
    tm = logits.shape[0]
    lane = lax.broadcasted_iota(jnp.int32, logits.shape, 1)
    lg = jnp.where(lane < N_EXPERTS, logits, -jnp.inf)
    m1 = jnp.max(lg, axis=1, keepdims=True)
    i1 = jnp.min(jnp.where(lg == m1, lane, LANE), axis=1, keepdims=True)
    lg2 = jnp.where(lane == i1, -jnp.inf, lg)
    m2 = jnp.max(lg2, axis=1, keepdims=True)
    i2 = jnp.min(jnp.where(lg2 == m2, lane, LANE), axis=1, keepdims=True)
    e2 = jnp.exp(m2 - m1)
    g1 = 1.0 / (1.0 + e2)
    g2 = e2 / (1.0 + e2)
    hit1 = lane == i1
    hit2 = lane == i2
    onehot = jnp.where(hit1 | hit2, 1.0, 0.0)
    ri = lax.broadcasted_iota(jnp.int32, (tm, tm), 0)
    ci = lax.broadcasted_iota(jnp.int32, (tm, tm), 1)
    before = jnp.where(ci < ri, 1.0, 0.0).astype(BF16)
    seen = carry_s[...] + _dot(before, onehot.astype(BF16))
    r1 = jnp.sum(jnp.where(hit1, seen, 0.0), axis=1, keepdims=True)
    r2 = jnp.sum(jnp.where(hit2, seen, 0.0), axis=1, keepdims=True)
    carry_s[...] += jnp.sum(onehot, axis=0, keepdims=True)
    idx = jnp.where(lane == 0, i1, jnp.where(lane == 1, i2, 0))
    rank = jnp.where(lane == 2, r1, jnp.where(lane == 3, r2, 0.0)).astype(jnp.int32)
    idx_out[...] = idx + rank
    gate_out[...] = jnp.where(lane == 0, g1, jnp.where(lane == 1, g2, 0.0))
    cnt_out[...] = carry_s[...]


def moe_router(x, g, wr, counts_in, tm):
    t, d = x.shape
    row = lambda i: (i, 0)
    full = lambda i: (0, 0)
    return pl.pallas_call(
        _router_kernel,
        grid=(t // tm,),
        in_specs=[pl.BlockSpec((tm, d), row), pl.BlockSpec((1, d), full), pl.BlockSpec((d, LANE), full),
                  pl.BlockSpec((1, LANE), full)],
        out_specs=[pl.BlockSpec((tm, LANE), row), pl.BlockSpec((tm, LANE), row),
                   pl.BlockSpec((1, LANE), full)],
        out_shape=[jax.ShapeDtypeStruct((t, LANE), jnp.int32),
                   jax.ShapeDtypeStruct((t, LANE), F32), jax.ShapeDtypeStruct((1, LANE), F32)],
        scratch_shapes=[pltpu.VMEM((1, LANE), F32)],
        compiler_params=_cparams(("arbitrary",), 40),
        name="moe_router",
    )(x, g, wr, counts_in)


def _dispatch_kernel(dest_ref, x_ref, init_ref, xs_ref, sem, *, tm):
    del init_ref
    i = pl.program_id(0)

    def row_copy(r, k):
        slot = dest_ref[2 * (i * tm + r) + k]
        return pltpu.make_async_copy(x_ref.at[pl.ds(r, 1), :], xs_ref.at[pl.ds(slot, 1), :], sem)

    def start(r, carry):
        row_copy(r, 0).start()
        row_copy(r, 1).start()
        return carry

    def wait(r, carry):
        row_copy(r, 0).wait()
        row_copy(r, 1).wait()
        return carry

    lax.fori_loop(0, tm, start, 0)
    lax.fori_loop(0, tm, wait, 0)


def moe_dispatch(dest_flat, x, init, tm):
    t, d = x.shape
    n_slots = init.shape[0]
    return pl.pallas_call(
        functools.partial(_dispatch_kernel, tm=tm),
        grid_spec=pltpu.PrefetchScalarGridSpec(
            num_scalar_prefetch=1,
            grid=(t // tm,),
            in_specs=[pl.BlockSpec((tm, d), lambda i, dr: (i, 0)), pl.BlockSpec(memory_space=pl.ANY)],
            out_specs=pl.BlockSpec(memory_space=pl.ANY),
            scratch_shapes=[pltpu.SemaphoreType.DMA(())]),
        out_shape=jax.ShapeDtypeStruct((n_slots, d), F32),
        input_output_aliases={2: 0},
        compiler_params=_cparams(("arbitrary",), 32),
        name="moe_dispatch",
    )(dest_flat, x, init)


def _expert_ffn_kernel(be_ref, nu_ref, x_ref, g_ref, wg_ref, wu_ref, wd_ref, o_ref, xn_ref, acc_ref):
    del be_ref
    i = pl.program_id(0)
    f = pl.program_id(1)
    used = i < nu_ref[0]

    @pl.when(f == 0)
    def _():
        xn_ref[...] = _rms(x_ref[...], g_ref[...], x_ref.shape[-1]).astype(BF16)
        acc_ref[...] = jnp.zeros(acc_ref.shape, F32)

    @pl.when(used)
    def _():
        x = xn_ref[...]
        a = jax.nn.silu(_dot(x, wg_ref[...])) * _dot(x, wu_ref[...].astype(BF16))
        acc_ref[...] += _dot(a.astype(BF16), wd_ref[...].astype(BF16))

    @pl.when(f == pl.num_programs(1) - 1)
    def _():
        o_ref[...] = acc_ref[...]


def expert_ffn(blk_e, n_used, xs, g, wg, wu, wd, blk, tf):
    p, d = xs.shape
    ff = wg.shape[2]
    nf = ff // tf

    def wcol(i, f, be, nu):
        live = i < nu[0]
        return (be[i], 0, jnp.where(live, f, nf - 1))

    def wrow(i, f, be, nu):
        live = i < nu[0]
        return (be[i], jnp.where(live, f, nf - 1), 0)

    return pl.pallas_call(
        _expert_ffn_kernel,
        grid_spec=pltpu.PrefetchScalarGridSpec(
            num_scalar_prefetch=2,
            grid=(p // blk, nf),
            in_specs=[pl.BlockSpec((blk, d), lambda i, f, be, nu: (i, 0)),
                      pl.BlockSpec((1, d), lambda i, f, be, nu: (0, 0)),
                      pl.BlockSpec((None, d, tf), wcol), pl.BlockSpec((None, d, tf), wcol),
                      pl.BlockSpec((None, tf, d), wrow)],
            out_specs=pl.BlockSpec((blk, d), lambda i, f, be, nu: (i, 0)),
            scratch_shapes=[pltpu.VMEM((blk, d), BF16), pltpu.VMEM((blk, d), F32)]),
        out_shape=jax.ShapeDtypeStruct((p, d), F32),
        compiler_params=_cparams(("arbitrary", "arbitrary"), 56),
        name="expert_ffn",
    )(blk_e, n_used, xs, g, wg, wu, wd)


def _combine_kernel(dest_ref, x_ref, gate_ref, ys_ref, o_ref, buf, sem):
    i = pl.program_id(0)
    n = pl.num_programs(0)
    tm = x_ref.shape[0]

    def row_copy(step, slot, r, k):
        src = dest_ref[2 * (step * tm + r) + k]
        return pltpu.make_async_copy(ys_ref.at[pl.ds(src, 1), :], buf.at[slot, k, pl.ds(r, 1), :], sem.at[slot])

    def start_block(step, slot):
        def body(r, carry):
            row_copy(step, slot, r, 0).start()
            row_copy(step, slot, r, 1).start()
            return carry
        lax.fori_loop(0, tm, body, 0)

    @pl.when(i == 0)
    def _():
        start_block(0, 0)

    @pl.when(i + 1 < n)
    def _():
        start_block(i + 1, (i + 1) % 2)

    slot = i % 2

    def wait_body(r, carry):
        row_copy(i, slot, r, 0).wait()
        row_copy(i, slot, r, 1).wait()
        return carry

    lax.fori_loop(0, tm, wait_body, 0)
    g = gate_ref[...]
    o_ref[...] = x_ref[...] + (buf[slot, 0] * g[:, 0:1] + buf[slot, 1] * g[:, 1:2])


def moe_combine(dest_flat, x, gates, ys, tm):
    t, d = x.shape
    return pl.pallas_call(
        _combine_kernel,
        grid_spec=pltpu.PrefetchScalarGridSpec(
            num_scalar_prefetch=1,
            grid=(t // tm,),
            in_specs=[pl.BlockSpec((tm, d), lambda i, dr: (i, 0)),
                      pl.BlockSpec((tm, LANE), lambda i, dr: (i, 0)),
                      pl.BlockSpec(memory_space=pl.ANY)],
            out_specs=pl.BlockSpec((tm, d), lambda i, dr: (i, 0)),
            scratch_shapes=[pltpu.VMEM((2, 2, tm, d), F32), pltpu.SemaphoreType.DMA((2,))]),
        out_shape=jax.ShapeDtypeStruct((t, d), F32),
        compiler_params=_cparams(("arbitrary",), 48),
        name="moe_combine",
    )(dest_flat, x, gates, ys)


def moe_mixer(xs_in, tms, g, wr, wg, wu, wd, blk, tf):
    d = xs_in[0].shape[1]
    counts = jnp.zeros((1, LANE), F32)
    routed = []
    for x, tm in zip(xs_in, tms):
        idx, gates, counts = moe_router(x, g, wr, counts, tm)
        routed.append((idx, gates))
    cnt = counts[0, :N_EXPERTS].astype(jnp.int32)
    padded = (cnt + blk - 1) // blk * blk
    pend = jnp.cumsum(padded)
    pstart = pend - padded
    n_rows = sum(x.shape[0] for x in xs_in)
    n_blocks = -(-(2 * n_rows) // blk) + N_EXPERTS
    n_used = (pend[-1] // blk).astype(jnp.int32)
    blk_ids = jnp.minimum(jnp.arange(n_blocks, dtype=jnp.int32), n_used - 1)
    blk_e = jnp.minimum(jnp.searchsorted(pend, blk_ids * blk, side="right"), N_EXPERTS - 1).astype(jnp.int32)
    dests = [(pstart[idx[:, 0:2]] + idx[:, 2:4]).reshape(-1).astype(jnp.int32) for idx, _ in routed]
    slots = jnp.zeros((n_blocks * blk, d), F32)
    for x, tm, dest in zip(xs_in, tms, dests):
        slots = moe_dispatch(dest, x, slots, tm)
    ys = expert_ffn(blk_e, n_used.reshape(1), slots, g, wg, wu, wd, blk, tf)
    return [moe_combine(dest, x, gates, ys, tm) for x, tm, dest, (_, gates) in zip(xs_in, tms, dests, routed)]


def _rope_tables(pos):
    half = A_ROPE // 2
    inv = ROPE_THETA ** (-jnp.arange(half, dtype=F32) / half)
    ang = pos.astype(F32)[:, None] * inv[None, :]
    cos, sin = jnp.cos(ang), jnp.sin(ang)
    zeros = jnp.zeros((pos.shape[0], LANE - A_ROPE), F32)
    return (jnp.concatenate([cos, cos, zeros], axis=-1), jnp.concatenate([-sin, sin, zeros], axis=-1))


def _pad_lanes(v, width=LANE):
    return jnp.pad(v, (0, width - v.shape[0])).reshape(1, width)


def _layer_weights(l, g_mix_norm, w_in, b_if, g_mh, g_cq, w_uq, g_ckv, w_uk, w_uv, g_qn, g_qr, g_kn, g_kr,
                   w_out, g_ffn_norm, g_ple_norm, w_ple_gate, w_ple_proj):
    d = w_in.shape[1]
    wi = w_in[l]
    o = 0
    parts = {}
    for name, width in (("mq", M_HEADS * M_DQK), ("mk", M_HEADS * M_DQK), ("mv", M_WIDTH), ("mo", M_WIDTH),
                        ("mi", M_HEADS), ("mf", M_HEADS), ("cq", Q_RANK), ("ckv", KV_RANK), ("kpe", A_ROPE)):
        parts[name] = wi[:, o:o + width]
        o += width
    used = COL_KG + A_ROPE + 2 * M_HEADS
    w_in_p = jnp.concatenate([parts[k] for k in ("mq", "mk", "mv", "mo", "cq", "ckv", "kpe", "mi", "mf")]
                             + [jnp.zeros((d, N_IN_PAD - used), F32)], axis=1).astype(BF16)
    wuq = w_uq[l].reshape(Q_RANK, A_HEADS, A_NOPE + A_ROPE)
    wuq = jnp.pad(wuq, ((0, 0), (0, 0), (0, QK_HEAD - A_NOPE - A_ROPE))).reshape(Q_RANK, A_HEADS * QK_HEAD)
    bias = jnp.zeros((LANE,), F32).at[GATE_I:GATE_I + 2 * M_HEADS].set(b_if[l].astype(F32)).reshape(1, LANE)
    wuk = w_uk[l].astype(BF16)
    return {
        "g_mix": g_mix_norm[l].reshape(1, d), "w_in": w_in_p, "bias": bias,
        "g_mh": g_mh[l].reshape(1, M_WIDTH), "g_cq": g_cq[l].reshape(1, Q_RANK),
        "w_uq": wuq.astype(BF16), "g_ckv": g_ckv[l].reshape(1, KV_RANK),
        "w_uk": wuk, "w_ukt": wuk.T, "w_uv": w_uv[l].astype(BF16),
        "g_qn": g_qn[l].reshape(1, A_NOPE), "g_qr": _pad_lanes(g_qr[l]),
        "g_kn": g_kn[l].reshape(1, A_NOPE), "g_kr": _pad_lanes(g_kr[l]),
        "w_out": w_out[l].astype(BF16), "g_ffn": g_ffn_norm[l].reshape(1, d),
        "g_ple": g_ple_norm[l].reshape(1, d), "w_ple_gate": w_ple_gate[l].astype(BF16),
        "w_ple_proj": w_ple_proj[l].astype(BF16),
    }


def _row_tile(t, pref):
    return pref if t % pref == 0 else t


def kernel(x_prompt, x_sample, p_prompt, p_sample, cache_ckv, cache_kpe, state_C, state_n, state_m, page_table, g_mix_norm, w_in, b_if, g_mh, g_cq, w_uq, g_ckv, w_uk, w_uv, g_qn, g_qr, g_kn, g_kr, w_out, g_ffn_norm, w_gate, w_up, w_down, w_router, w_gate_e, w_up_e, w_down_e, g_ple_norm, w_ple_gate, w_ple_proj):
    depth = w_in.shape[0]
    bp, sp, d = x_prompt.shape
    bs, ss, _ = x_sample.shape
    n_pages = page_table.shape[1]
    past = n_pages * PAGE_SIZE
    tp = bp * sp
    ts = bs * ss
    tm_p = _row_tile(sp, 512)
    tm_s = ts
    chunk = next((c for c in (MLSTM_CHUNK, M_CHUNK) if sp % c == 0), sp)
    pad_s = 16

    cos_p, sin_p = _rope_tables(jnp.arange(sp))
    cos_s, sin_s = _rope_tables(past + jnp.arange(ss))
    cos_s = jnp.broadcast_to(cos_s[None], (bs, ss, LANE)).reshape(ts, LANE)
    sin_s = jnp.broadcast_to(sin_s[None], (bs, ss, LANE)).reshape(ts, LANE)

    cache_kpe_t = jnp.swapaxes(cache_kpe, 2, 3)
    y_p = x_prompt.reshape(tp, d)
    y_s = x_sample.reshape(ts, d)
    outs = {k: [] for k in ("ckv_p", "kpe_p", "C_p", "n_p", "m_p", "ckv_s", "kpe_s", "C_s", "n_s", "m_s")}
    zc = jnp.zeros((1, bp, M_HEADS, M_DQK, M_DV), F32)
    group_s = next(g for g in (4, 2, 1) if ts % g == 0)
    zn = jnp.zeros((bp, M_HEADS, 1, M_DQK), F32)
    zm = jnp.zeros((bp, M_HEADS, 1, LANE), F32)

    for l in range(depth):
        lw = _layer_weights(l, g_mix_norm, w_in, b_if, g_mh, g_cq, w_uq, g_ckv, w_uk, w_uv, g_qn, g_qr, g_kn,
                            g_kr, w_out, g_ffn_norm, g_ple_norm, w_ple_gate, w_ple_proj)

        z_p = norm_matmul(y_p, lw["g_mix"], lw["w_in"], tm_p, 2048)
        ckv_p, kpe_p, q_p, k_p, v_p = mla_prep(z_p, cos_p, sin_p, lw, tm_p, sp // tm_p, False)
        hm_p, c1, n1, m1 = mlstm(z_p, lw["bias"], lw["g_mh"], zc, zn, zm, 0, bp, sp // chunk, chunk, chunk, 1)
        ha_p = prompt_attention(q_p, k_p, v_p, bp, sp, tm_p)
        y_p = out_proj(hm_p, ha_p, lw["w_out"], y_p, tm_p, 2048)
        outs["ckv_p"].append(ckv_p.reshape(bp, sp, KV_RANK))
        outs["kpe_p"].append(kpe_p.reshape(bp, sp, A_ROPE))
        outs["C_p"].append(c1)
        outs["n_p"].append(n1.reshape(bp, M_HEADS, M_DQK))
        outs["m_p"].append(m1[:, :, 0, 0])

        z_s = norm_matmul(y_s, lw["g_mix"], lw["w_in"], tm_s, 1024)
        ckv_s, kpe_s, q_s, k_s, _, a_s = mla_prep(z_s, cos_s, sin_s, lw, tm_s, 1, True)
        z_s8 = jnp.pad(z_s.reshape(ts, 1, N_IN_PAD), ((0, 0), (pad_s - 1, 0), (0, 0))).reshape(ts * pad_s, N_IN_PAD)
        hm_s8, c2, n2, m2 = mlstm(z_s8, lw["bias"], lw["g_mh"], state_C,
                                  state_n[l].reshape(bs, M_HEADS, 1, M_DQK),
                                  jnp.broadcast_to(state_m[l][:, :, None, None], (bs, M_HEADS, 1, LANE)),
                                  l, ts, 1, pad_s, 1, group_s)
        outs["C_s"].append(c2)
        hm_s = hm_s8.reshape(ts, pad_s, M_WIDTH)[:, pad_s - 1, :]
        ha_s = decode_attention(l, page_table, cache_ckv, cache_kpe_t,
                                q_s.reshape(ts, A_HEADS, QK_HEAD), k_s.reshape(ts, A_HEADS, QK_HEAD),
                                a_s.reshape(ts, A_HEADS, KV_RANK), ckv_s.reshape(ts, 1, KV_RANK),
                                lw["w_ukt"], lw["w_uv"], next(p for p in (16, 8, 1) if n_pages % p == 0))
        y_s = out_proj(hm_s, ha_s.reshape(ts, A_WIDTH).astype(BF16), lw["w_out"], y_s, tm_s, 1024)
        outs["ckv_s"].append(ckv_s.reshape(bs, ss, KV_RANK))
        outs["kpe_s"].append(kpe_s.reshape(bs, ss, A_ROPE))
        outs["n_s"].append(n2.reshape(bs, M_HEADS, M_DQK))
        outs["m_s"].append(m2[:, :, 0, 0])

        if l % 2 == 0:
            wg, wu, wd = (w[l // 2].astype(BF16) for w in (w_gate, w_up, w_down))
            tf = 512 if wg.shape[1] % 512 == 0 else wg.shape[1]
            y_p = ffn_dense(y_p, lw["g_ffn"], wg, wu, wd, tm_p, tf)
            y_s = ffn_dense(y_s, lw["g_ffn"], wg, wu, wd, tm_s, tf)
        else:
            wg, wu, wd = w_gate_e[l // 2].astype(BF16), w_up_e[l // 2], w_down_e[l // 2]
            wr = jnp.pad(w_router[l // 2], ((0, 0), (0, LANE - N_EXPERTS))).astype(BF16)
            tf = 512 if wg.shape[2] % 512 == 0 else wg.shape[2]
            y_p, y_s = moe_mixer([y_p, y_s], [min(tm_p, 256), tm_s], lw["g_ffn"], wr, wg, wu, wd, tm_p, tf)
        y_p = ple(y_p, lw["g_ple"], p_prompt[l].reshape(tp, -1), lw["w_ple_gate"], lw["w_ple_proj"], tm_p, 2048)
        y_s = ple(y_s, lw["g_ple"], p_sample[l].reshape(ts, -1), lw["w_ple_gate"], lw["w_ple_proj"], tm_s, 1024)

    st = lambda k: jnp.stack(outs[k])
    return (y_p.reshape(bp, sp, d), y_s.reshape(bs, ss, d),
            st("ckv_p"), st("kpe_p"), st("C_p"), st("n_p"), st("m_p"),
            st("ckv_s"), st("kpe_s"), st("C_s"), st("n_s"), st("m_s"))
```
